```python
import math
import jax
import jax.numpy as jnp
from jax import lax
import numpy as np

D_MODEL = 4096
BATCH = 2
SEQ = 4096
DEPTH = 1

GRID_W = 64
CTX_LEN = 256

GDN_HEADS = D_MODEL // 256
GDN_DK = 128
GDN_DV = 128
GDN_CONV = 5
GDN_CHUNK = 64

DIFF_HEADS = D_MODEL // 256
DIFF_DQK = 64
DIFF_DV = 2 * DIFF_DQK
Q_BLOCK = 128
ROPE_BASE = 10000.0

D_FF = -(-8 * D_MODEL // 768) * 256

GDN_QK_W = GDN_HEADS * GDN_DK
GDN_V_W = GDN_HEADS * GDN_DV
GDN_QKV_W = 2 * GDN_QK_W + GDN_V_W
DIFF_QK_W = DIFF_HEADS * 2 * DIFF_DQK
DIFF_V_W = DIFF_HEADS * DIFF_DV
IN_SPLITS = (GDN_QKV_W, GDN_V_W, 2 * GDN_HEADS, 2 * GDN_HEADS, DIFF_QK_W, DIFF_QK_W, DIFF_V_W, D_MODEL, D_MODEL)
D_IN = sum(IN_SPLITS)

ALPHA = (2.0 * DEPTH) ** 0.25
BETA_INIT = (8.0 * DEPTH) ** -0.25
LN_EPS = 1e-6

kernel_name = 'hybrid_gdn_diffattn_dit_block'


def _split_in(p):
    idx = []
    acc = 0
    for w in IN_SPLITS[:-1]:
        acc += w
        idx.append(acc)
    return jnp.split(p, idx, axis=-1)


def _layernorm(x, g, b):
    xf = x.astype(jnp.float32)
    mu = jnp.mean(xf, axis=-1, keepdims=True)
    var = jnp.mean(jnp.square(xf - mu), axis=-1, keepdims=True)
    y = (xf - mu) * lax.rsqrt(var + LN_EPS) * g.astype(jnp.float32) + b.astype(jnp.float32)
    return y.astype(x.dtype)


def _rmsnorm(x, w, eps):
    xf = x.astype(jnp.float32)
    return xf * lax.rsqrt(jnp.mean(jnp.square(xf), axis=-1, keepdims=True) + eps) * w.astype(jnp.float32)


def _l2norm(x):
    xf = x.astype(jnp.float32)
    return xf * lax.rsqrt(jnp.sum(jnp.square(xf), axis=-1, keepdims=True) + 1e-6)


def _short_conv(x, w):
    ch = x.shape[-1]
    pad = w.shape[0] // 2
    y = lax.conv_general_dilated(x, w[:, None, :].astype(x.dtype), window_strides=(1,),
                                 padding=((pad, pad),), dimension_numbers=('NWC', 'WIO', 'NWC'),
                                 feature_group_count=ch)
    return jax.nn.silu(y)


def _rope_1d(x, pos):
    half = x.shape[-1] // 2
    inv_freq = ROPE_BASE ** (-jnp.arange(half, dtype=jnp.float32) / half)
    ang = pos.astype(jnp.float32)[:, None] * inv_freq[None, :]
    cos = jnp.cos(ang)[None, :, None, None, :]
    sin = jnp.sin(ang)[None, :, None, None, :]
    xf = x.astype(jnp.float32)
    x1, x2 = xf[..., :half], xf[..., half:]
    return jnp.concatenate([x1 * cos - x2 * sin, x2 * cos + x1 * sin], axis=-1)


def _rope_axial(x, pos_row, pos_col):
    d = x.shape[-1] // 2
    y = jnp.concatenate([_rope_1d(x[..., :d], pos_row), _rope_1d(x[..., d:], pos_col)], axis=-1)
    return y.astype(x.dtype)


def _gated_delta_rule(q, k, v, g, beta, state0):
    B, T, H, _ = q.shape
    dv = v.shape[-1]
    n = T // GDN_CHUNK
    f32 = jnp.float32

    def blocks(t):
        return t.astype(f32).reshape(B, n, GDN_CHUNK, H, -1).transpose(1, 0, 3, 2, 4)

    qc, kc, vc = blocks(q), blocks(k), blocks(v)
    gc = g.astype(f32).reshape(B, n, GDN_CHUNK, H).transpose(1, 0, 3, 2)
    bc = beta.astype(f32).reshape(B, n, GDN_CHUNK, H).transpose(1, 0, 3, 2)
    G = jnp.cumsum(gc, axis=-1)
    idx = jnp.arange(GDN_CHUNK)
    lower = idx[:, None] >= idx[None, :]
    strict = idx[:, None] > idx[None, :]
    gdiff = G[..., :, None] - G[..., None, :]
    decay = jnp.where(lower, jnp.exp(jnp.where(lower, gdiff, 0.0)), 0.0)
    kb = kc * bc[..., None]
    vb = vc * bc[..., None]
    a_mat = jnp.where(strict, jnp.einsum('nbhid,nbhjd->nbhij', kb, kc) * decay, 0.0)
    l_mat = a_mat + jnp.eye(GDN_CHUNK, dtype=f32)
    u = lax.linalg.triangular_solve(l_mat, vb, left_side=True, lower=True, unit_diagonal=True)
    w = lax.linalg.triangular_solve(l_mat, kb * jnp.exp(G)[..., None], left_side=True, lower=True,
                                    unit_diagonal=True)
    qk = jnp.where(lower, jnp.einsum('nbhid,nbhjd->nbhij', qc, kc) * decay, 0.0)
    q_dec = qc * jnp.exp(G)[..., None]
    k_dec = kc * jnp.exp(G[..., -1:] - G)[..., None]
    g_tot = jnp.exp(G[..., -1])

    def step(state, xs):
        qk_i, qd_i, w_i, u_i, kd_i, gt_i = xs
        v_new = u_i - jnp.einsum('bhcd,bhde->bhce', w_i, state)
        o_i = jnp.einsum('bhcd,bhde->bhce', qd_i, state) + jnp.einsum('bhij,bhje->bhie', qk_i, v_new)
        state = state * gt_i[..., None, None] + jnp.einsum('bhcd,bhce->bhde', kd_i, v_new)
        return state, o_i

    state, o = lax.scan(step, state0.astype(f32), (qk, q_dec, w, u, k_dec, g_tot))
    o = o.transpose(1, 0, 3, 2, 4).reshape(B, T, H, dv)
    return o, state


def _gdn_prep(qkv, a, b, conv_w, a_log, dt_bias):
    B, T, _ = qkv.shape
    qkv = _short_conv(qkv, conv_w)
    q, k, v = jnp.split(qkv, [GDN_QK_W, 2 * GDN_QK_W], axis=-1)
    q = _l2norm(q.reshape(B, T, GDN_HEADS, GDN_DK)) * (GDN_DK ** -0.5)
    k = _l2norm(k.reshape(B, T, GDN_HEADS, GDN_DK))
    v = v.reshape(B, T, GDN_HEADS, GDN_DV)
    af = a.astype(jnp.float32).reshape(B, T, 2, GDN_HEADS)
    g = -jnp.exp(a_log.astype(jnp.float32)) * jax.nn.softplus(af + dt_bias.astype(jnp.float32))
    beta = jax.nn.sigmoid(b.astype(jnp.float32)).reshape(B, T, 2, GDN_HEADS)
    return q, k, v, g, beta


def _bidir_gdn(lat, ctxs):
    q, k, v, g, beta = lat
    qc, kc, vc, gc, bc = ctxs
    B = q.shape[0]
    zero = jnp.zeros((B, GDN_HEADS, GDN_DK, GDN_DV), jnp.float32)

    def rev(t):
        return jnp.flip(t, axis=1)

    oc_f, sc_f = _gated_delta_rule(qc, kc, vc, gc[:, :, 0], bc[:, :, 0], zero)
    o_f, _ = _gated_delta_rule(q, k, v, g[:, :, 0], beta[:, :, 0], sc_f)
    oc_b, sc_b = _gated_delta_rule(rev(qc), rev(kc), rev(vc), rev(gc[:, :, 1]), rev(bc[:, :, 1]), zero)
    o_b, _ = _gated_delta_rule(rev(q), rev(k), rev(v), rev(g[:, :, 1]), rev(beta[:, :, 1]), sc_b)
    return o_f + rev(o_b), oc_f + rev(oc_b)


def _gdn_out(o, z, norm_w, dtype):
    B, T, H, dv = o.shape
    y = _rmsnorm(o, norm_w, 1e-6) * jax.nn.silu(z.astype(jnp.float32)).reshape(B, T, H, dv)
    return y.reshape(B, T, H * dv).astype(dtype)


def _diff_heads(qp, kp, vp):
    B, T, _ = qp.shape
    q = qp.reshape(B, T, DIFF_HEADS, 2, DIFF_DQK)
    k = kp.reshape(B, T, DIFF_HEADS, 2, DIFF_DQK)
    v = vp.reshape(B, T, DIFF_HEADS, DIFF_DV)
    return q, k, v


def _diff_attend(q, k_all, v_all, lam):
    s = jnp.einsum('bqhmd,bkhmd->bhmqk', q, k_all, preferred_element_type=jnp.float32) * (DIFF_DQK ** -0.5)
    p = jax.nn.softmax(s, axis=-1)
    p = p[:, :, 0] - lam * p[:, :, 1]
    return jnp.einsum('bhqk,bkhd->bqhd', p.astype(v_all.dtype), v_all)


def _diff_out(o, norm_w, lam_init, dtype):
    B, T = o.shape[0], o.shape[1]
    y = _rmsnorm(o, norm_w, 1e-5) * (1.0 - lam_init)
    return y.reshape(B, T, DIFF_V_W).astype(dtype)


def _merge(y_a, y_b, gate_a, gate_b, w_proj_a, w_proj_b, w_out):
    y = jax.nn.sigmoid(gate_a) * (y_a @ w_proj_a) + jax.nn.sigmoid(gate_b) * (y_b @ w_proj_b)
    return y @ w_out


def _hybrid_mixer(h, hc, pos_row, pos_col, lam_init, with_ctx, w_in, conv_w, a_log, dt_bias, gdn_norm_w,
                  lam_q1, lam_k1, lam_q2, lam_k2, diff_norm_w, w_proj_a, w_proj_b, w_out):
    B, S, _ = h.shape
    qkv_a, z_a, a_a, b_a, q_b, k_b, v_b, gate_a, gate_b = _split_in(h @ w_in)
    qkv_ac, z_ac, a_ac, b_ac, q_bc, k_bc, v_bc, gate_ac, gate_bc = _split_in(hc @ w_in)

    lat_a = _gdn_prep(qkv_a, a_a, b_a, conv_w, a_log, dt_bias)
    ctx_a = _gdn_prep(qkv_ac, a_ac, b_ac, conv_w, a_log, dt_bias)
    o_a, o_ac = _bidir_gdn(lat_a, ctx_a)
    y_a = _gdn_out(o_a, z_a, gdn_norm_w, h.dtype)

    f32 = jnp.float32
    lam = (jnp.exp(jnp.sum(lam_q1.astype(f32) * lam_k1.astype(f32)))
           - jnp.exp(jnp.sum(lam_q2.astype(f32) * lam_k2.astype(f32))) + lam_init)
    q, k, v = _diff_heads(q_b, k_b, v_b)
    qc, kc, vc = _diff_heads(q_bc, k_bc, v_bc)
    q = _rope_axial(q, pos_row, pos_col)
    k = _rope_axial(k, pos_row, pos_col)
    k_all = jnp.concatenate([k, kc], axis=1)
    v_all = jnp.concatenate([v, vc], axis=1)
    q_blocks = q.reshape(B, S // Q_BLOCK, Q_BLOCK, DIFF_HEADS, 2, DIFF_DQK).transpose(1, 0, 2, 3, 4, 5)
    o_b = lax.map(lambda qb: _diff_attend(qb, k_all, v_all, lam), q_blocks)
    o_b = o_b.transpose(1, 0, 2, 3, 4).reshape(B, S, DIFF_HEADS, DIFF_DV)
    y_b = _diff_out(o_b, diff_norm_w, lam_init, h.dtype)

    out = _merge(y_a, y_b, gate_a, gate_b, w_proj_a, w_proj_b, w_out)
    if not with_ctx:
        return out, None
    y_ac = _gdn_out(o_ac, z_ac, gdn_norm_w, hc.dtype)
    o_bc = _diff_attend(qc, kc, vc, lam)
    y_bc = _diff_out(o_bc, diff_norm_w, lam_init, hc.dtype)
    out_c = _merge(y_ac, y_bc, gate_ac, gate_bc, w_proj_a, w_proj_b, w_out)
    return out, out_c


def _swiglu(h, w_gate, w_up, w_down):
    return (jax.nn.silu(h @ w_gate) * (h @ w_up)) @ w_down


def setup_inputs(seed: int = 0) -> dict:
    key = jax.random.key(seed)
    ks = jax.random.split(key, 32)
    f32 = jnp.float32
    L = DEPTH

    def nrm(k, shape, scale):
        return jax.random.normal(k, shape, f32) * scale

    x = nrm(ks[0], (BATCH, SEQ, D_MODEL), 1.0)
    c = nrm(ks[1], (BATCH, D_MODEL), 1.0)
    ctx = nrm(ks[2], (BATCH, CTX_LEN, D_MODEL), 1.0)
    c_ctx = nrm(ks[3], (D_MODEL,), 1.0)
    w_ada = nrm(ks[4], (L, D_MODEL, 6 * D_MODEL), D_MODEL ** -0.5)
    b_ada = nrm(ks[5], (L, 6 * D_MODEL), 0.02)
    w_in = nrm(ks[6], (L, D_MODEL, D_IN), D_MODEL ** -0.5)
    conv_w = nrm(ks[7], (L, GDN_CONV, GDN_QKV_W), GDN_CONV ** -0.5)
    a_log = jnp.log(jax.random.uniform(ks[8], (L, 2, GDN_HEADS), f32, 1.0, 16.0))
    dt = jnp.exp(jax.random.uniform(ks[9], (L, 2, GDN_HEADS), f32, math.log(1e-3), math.log(1e-1)))
    dt_bias = dt + jnp.log(-jnp.expm1(-dt))
    gdn_norm_w = 1.0 + nrm(ks[10], (L, GDN_DV), 0.02)
    lam_q1 = nrm(ks[11], (L, DIFF_DQK), 0.1)
    lam_k1 = nrm(ks[12], (L, DIFF_DQK), 0.1)
    lam_q2 = nrm(ks[13], (L, DIFF_DQK), 0.1)
    lam_k2 = nrm(ks[14], (L, DIFF_DQK), 0.1)
    diff_norm_w = 1.0 + nrm(ks[15], (L, DIFF_DV), 0.02)
    w_proj_a = nrm(ks[16], (L, GDN_V_W, D_MODEL), GDN_V_W ** -0.5)
    w_proj_b = nrm(ks[17], (L, DIFF_V_W, D_MODEL), DIFF_V_W ** -0.5)
    w_out = nrm(ks[18], (L, D_MODEL, D_MODEL), (D_MODEL ** -0.5) * BETA_INIT)
    ln1_g = 1.0 + nrm(ks[19], (L, D_MODEL), 0.02)
    ln1_b = nrm(ks[20], (L, D_MODEL), 0.02)
    w_gate = nrm(ks[21], (L, D_MODEL, D_FF), D_MODEL ** -0.5)
    w_up = nrm(ks[22], (L, D_MODEL, D_FF), D_MODEL ** -0.5)
    w_down = nrm(ks[23], (L, D_FF, D_MODEL), (D_FF ** -0.5) * BETA_INIT)
    ln2_g = 1.0 + nrm(ks[24], (L, D_MODEL), 0.02)
    ln2_b = nrm(ks[25], (L, D_MODEL), 0.02)
    return {'x': x, 'c': c, 'ctx': ctx, 'c_ctx': c_ctx, 'w_ada': w_ada, 'b_ada': b_ada, 'w_in': w_in,
            'conv_w': conv_w, 'a_log': a_log, 'dt_bias': dt_bias, 'gdn_norm_w': gdn_norm_w,
            'lam_q1': lam_q1, 'lam_k1': lam_k1, 'lam_q2': lam_q2, 'lam_k2': lam_k2,
            'diff_norm_w': diff_norm_w, 'w_proj_a': w_proj_a, 'w_proj_b': w_proj_b, 'w_out': w_out,
            'ln1_g': ln1_g, 'ln1_b': ln1_b, 'w_gate': w_gate, 'w_up': w_up, 'w_down': w_down,
            'ln2_g': ln2_g, 'ln2_b': ln2_b}


def reference(x, c, ctx, c_ctx, w_ada, b_ada, w_in, conv_w, a_log, dt_bias, gdn_norm_w, lam_q1, lam_k1,
              lam_q2, lam_k2, diff_norm_w, w_proj_a, w_proj_b, w_out, ln1_g, ln1_b, w_gate, w_up, w_down,
              ln2_g, ln2_b):
    S = x.shape[1]
    ROWS = S // GRID_W
    t = jnp.arange(ROWS * GRID_W, dtype=jnp.int32)
    pos_row = t // GRID_W
    pos_col = t % GRID_W

    xl, xc = x, ctx
    for layer in range(DEPTH):
        last = layer == DEPTH - 1
        lam_init = 0.8 - 0.6 * math.exp(-0.3 * layer)
        mod = jax.nn.silu(c) @ w_ada[layer] + b_ada[layer]
        mod_c = jax.nn.silu(c_ctx) @ w_ada[layer] + b_ada[layer]
        sh_a, sc_a, gt_a, sh_f, sc_f, gt_f = [m[:, None, :] for m in jnp.split(mod, 6, axis=-1)]
        shc_a, scc_a, gtc_a, shc_f, scc_f, gtc_f = jnp.split(mod_c, 6, axis=-1)

        mix, mix_c = _hybrid_mixer(xl * (1.0 + sc_a) + sh_a, xc * (1.0 + scc_a) + shc_a, pos_row, pos_col,
                                   lam_init, not last, w_in[layer], conv_w[layer], a_log[layer],
                                   dt_bias[layer], gdn_norm_w[layer], lam_q1[layer], lam_k1[layer],
                                   lam_q2[layer], lam_k2[layer], diff_norm_w[layer], w_proj_a[layer],
                                   w_proj_b[layer], w_out[layer])
        xl = _layernorm(ALPHA * xl + gt_a * mix, ln1_g[layer], ln1_b[layer])
        ffn = _swiglu(xl * (1.0 + sc_f) + sh_f, w_gate[layer], w_up[layer], w_down[layer])
        xl = _layernorm(ALPHA * xl + gt_f * ffn, ln2_g[layer], ln2_b[layer])

        if not last:
            xc = _layernorm(ALPHA * xc + gtc_a * mix_c, ln1_g[layer], ln1_b[layer])
            ffn_c = _swiglu(xc * (1.0 + scc_f) + shc_f, w_gate[layer], w_up[layer], w_down[layer])
            xc = _layernorm(ALPHA * xc + gtc_f * ffn_c, ln2_g[layer], ln2_b[layer])
    return xl
```

```python
import functools
import math

import jax
import jax.numpy as jnp
from jax import lax
from jax.experimental import pallas as pl
from jax.experimental.pallas import tpu as pltpu

F32 = jnp.float32
BF16 = jnp.bfloat16

GRID_W = 64
HEAD_W = 256
GDN_DK = 128
GDN_DV = 128
GDN_CHUNK = 64
DIFF_DQK = 64
DIFF_DV = 128
ROPE_BASE = 10000.0
LN_EPS = 1e-6
LAM_INIT = 0.8 - 0.6 * math.exp(-0.3 * 0)

LANES = 128
SUBLANES = 8
PAIR = 2 * GDN_CHUNK
ROW_BLOCK = 256
VMEM_LIMIT = 56 * 1024 * 1024

_NT = (((1,), (1,)), ((), ()))


def _params(*sem):
    return pltpu.CompilerParams(dimension_semantics=sem, vmem_limit_bytes=VMEM_LIMIT)


def _tile(n, target, quantum):
    best = None
    t = quantum
    while t <= min(n, target):
        if n % t == 0:
            best = t
        t += quantum
    assert best is not None, (n, target, quantum)
    return best


def _silu(x):
    return x * jax.nn.sigmoid(x)


def _ada_kernel(c_ref, w_ref, b_ref, o_ref):
    c = _silu(c_ref[...])
    o_ref[...] = jnp.dot(c.astype(BF16), w_ref[...].astype(BF16),
                         preferred_element_type=F32) + b_ref[...]


def _ada(c8, w_ada, b_ada):
    d, n = w_ada.shape
    tn = _tile(n, 512, LANES)
    return pl.pallas_call(
        _ada_kernel,
        grid=(n // tn,),
        in_specs=[pl.BlockSpec((SUBLANES, d), lambda j: (0, 0)),
                  pl.BlockSpec((d, tn), lambda j: (0, j)),
                  pl.BlockSpec((1, tn), lambda j: (0, j))],
        out_specs=pl.BlockSpec((SUBLANES, tn), lambda j: (0, j)),
        out_shape=jax.ShapeDtypeStruct((SUBLANES, n), F32),
        compiler_params=_params("parallel"),
        name="ada",
    )(c8, w_ada, b_ada.reshape(1, n))


def _modulate_kernel(x_ref, ctx_ref, sh_ref, sc_ref, o_ref, *, n_lat, ctx_row):
    b = pl.program_id(0)
    i = pl.program_id(1)

    @pl.when(i < n_lat)
    def _():
        sc = sc_ref[pl.ds(b, 1), :]
        sh = sh_ref[pl.ds(b, 1), :]
        o_ref[0] = (x_ref[0] * (1.0 + sc) + sh).astype(BF16)

    @pl.when(i >= n_lat)
    def _():
        sc = sc_ref[ctx_row:ctx_row + 1, :]
        sh = sh_ref[ctx_row:ctx_row + 1, :]
        o_ref[0] = (ctx_ref[0] * (1.0 + sc) + sh).astype(BF16)


def _modulate(x, ctx, mod, ctx_row):
    bsz, s, d = x.shape
    c = ctx.shape[1]
    t = s + c
    rb = ROW_BLOCK
    n_lat = s // rb
    return pl.pallas_call(
        functools.partial(_modulate_kernel, n_lat=n_lat, ctx_row=ctx_row),
        grid=(bsz, t // rb),
        in_specs=[pl.BlockSpec((1, rb, d), lambda b, i: (b, jnp.minimum(i, n_lat - 1), 0)),
                  pl.BlockSpec((1, rb, d), lambda b, i: (b, jnp.maximum(i - n_lat, 0), 0)),
                  pl.BlockSpec((SUBLANES, d), lambda b, i: (0, 0)),
                  pl.BlockSpec((SUBLANES, d), lambda b, i: (0, 1))],
        out_specs=pl.BlockSpec((1, rb, d), lambda b, i: (b, i, 0)),
        out_shape=jax.ShapeDtypeStruct((bsz, t, d), BF16),
        compiler_params=_params("parallel", "arbitrary"),
        name="modulate",
    )(x, ctx, mod, mod)


def _matmul_kernel(a_ref, w_ref, o_ref):
    o_ref[...] = jnp.dot(a_ref[...], w_ref[...], preferred_element_type=F32).astype(o_ref.dtype)


def _matmul(a, w, *, tm, tn, name, out_dtype=F32):
    m, k = a.shape
    n = w.shape[1]
    return pl.pallas_call(
        _matmul_kernel,
        grid=(n // tn, m // tm),
        in_specs=[pl.BlockSpec((tm, k), lambda j, i: (i, 0)),
                  pl.BlockSpec((k, tn), lambda j, i: (0, j))],
        out_specs=pl.BlockSpec((tm, tn), lambda j, i: (i, j)),
        out_shape=jax.ShapeDtypeStruct((m, n), out_dtype),
        compiler_params=_params("parallel", "arbitrary"),
        name=name,
    )(a, w)


def _gates_kernel(p_ref, alog_ref, dtb_ref, o_ref, *, w2):
    p = p_ref[0]
    rb = p.shape[0]
    lane = lax.broadcasted_iota(jnp.int32, p.shape, 1)
    row = lax.broadcasted_iota(jnp.int32, p.shape, 0) % GDN_CHUNK
    xa = p + dtb_ref[...]
    softplus = jnp.maximum(xa, 0.0) + jnp.log(1.0 + jnp.exp(-jnp.abs(xa)))
    g = jnp.where(lane < w2, -jnp.exp(alog_ref[...]) * softplus, 0.0)
    pre = g
    suf = g
    step = 1
    while step < GDN_CHUNK:
        pre = pre + jnp.where(row >= step, pltpu.roll(pre, step, axis=0), 0.0)
        suf = suf + jnp.where(row < GDN_CHUNK - step, pltpu.roll(suf, rb - step, axis=0), 0.0)
        step *= 2
    tot = pre + suf - g
    cum = jnp.where(lane < w2 // 2, pre, suf)
    beta = jax.nn.sigmoid(p)
    out = jnp.where(lane < w2, cum,
                    jnp.where(lane < 2 * w2, beta,
                              jnp.where(lane < 3 * w2, pltpu.roll(tot, 2 * w2, axis=1),
                                        pltpu.roll(jnp.exp(tot), 3 * w2, axis=1))))
    o_ref[0] = jnp.where(lane < 4 * w2, out, 0.0)


def _gates(p_ab, alog128, dtb128, w2):
    bsz, t, _ = p_ab.shape
    rb = ROW_BLOCK
    return pl.pallas_call(
        functools.partial(_gates_kernel, w2=w2),
        grid=(bsz, t // rb),
        in_specs=[pl.BlockSpec((1, rb, LANES), lambda b, i: (b, i, 0)),
                  pl.BlockSpec((1, LANES), lambda b, i: (0, 0)),
                  pl.BlockSpec((1, LANES), lambda b, i: (0, 0))],
        out_specs=pl.BlockSpec((1, rb, LANES), lambda b, i: (b, i, 0)),
        out_shape=jax.ShapeDtypeStruct((bsz, t, LANES), F32),
        compiler_params=_params("parallel", "parallel"),
        name="gdn_gates",
    )(p_ab, alog128, dtb128)


def _conv_kernel(x_ref, cw_ref, o_ref, xp_ref, *, n_heads, n_lat, n_blk, taps):
    j = pl.program_id(1)
    t = x_ref.shape[1]
    rb = ROW_BLOCK
    halo = SUBLANES
    pad = taps // 2
    win_rows = rb + 2 * halo
    xp_ref[0:halo, :] = jnp.zeros((halo, LANES), F32)
    xp_ref[halo + t:2 * halo + t, :] = jnp.zeros((halo, LANES), F32)
    xp_ref[halo:halo + t, :] = x_ref[0]
    w = cw_ref[...]
    ridx = lax.broadcasted_iota(jnp.int32, (win_rows, LANES), 0)
    is_qk = j < 2 * n_heads
    q_scale = jnp.where(j < n_heads, GDN_DK ** -0.5, 1.0)

    def body(r, carry):
        base = pl.multiple_of(r * rb, rb)
        win = xp_ref[pl.ds(base, win_rows), :]
        top_ok = jnp.logical_and(r != 0, r != n_lat)
        bot_ok = jnp.logical_and(r != n_lat - 1, r != n_blk - 1)
        kill = jnp.logical_or(jnp.logical_and(ridx < halo, jnp.logical_not(top_ok)),
                              jnp.logical_and(ridx >= halo + rb, jnp.logical_not(bot_ok)))
        win = jnp.where(kill, 0.0, win)
        acc = jnp.zeros((win_rows, LANES), F32)
        for tap in range(taps):
            shift = (pad - tap) % win_rows
            shifted = win if shift == 0 else pltpu.roll(win, shift, axis=0)
            acc = acc + shifted * w[tap:tap + 1, :]
        y = _silu(acc[halo:halo + rb, :])
        ss = jnp.sum(y * y, axis=-1, keepdims=True)
        scale = jnp.where(is_qk, lax.rsqrt(ss + 1e-6) * q_scale, 1.0)
        o_ref[0, 0, pl.ds(base, rb), :] = y * scale
        return carry

    lax.fori_loop(0, n_blk, body, 0)


def _gdn_conv(p3, conv_w, n_heads, s):
    bsz, t, _ = p3.shape
    taps = conv_w.shape[0]
    n_cols = 3 * n_heads
    return pl.pallas_call(
        functools.partial(_conv_kernel, n_heads=n_heads, n_lat=s // ROW_BLOCK,
                          n_blk=t // ROW_BLOCK, taps=taps),
        grid=(bsz, n_cols),
        in_specs=[pl.BlockSpec((1, t, LANES), lambda b, j: (b, 0, j)),
                  pl.BlockSpec((taps, LANES), lambda b, j: (0, j))],
        out_specs=pl.BlockSpec((1, 1, t, LANES), lambda b, j: (b, j, 0, 0)),
        out_shape=jax.ShapeDtypeStruct((bsz, n_cols, t, LANES), F32),
        scratch_shapes=[pltpu.VMEM((t + 2 * SUBLANES, LANES), F32)],
        compiler_params=_params("parallel", "parallel"),
        name="gdn_conv",
    )(p3, conv_w)


def _gdn_intra_kernel(q_ref, k_ref, v_ref, gb_ref, gr_ref,
                      u_ref, w_ref, qd_ref, qk_ref, kdt_ref, *, n_heads):
    h = pl.program_id(1)
    w2 = 2 * n_heads
    rb = q_ref.shape[2]
    gbv = gb_ref[0]
    lane = lax.broadcasted_iota(jnp.int32, gbv.shape, 1)

    def column(idx):
        return jnp.sum(jnp.where(lane == idx, gbv, 0.0), axis=-1, keepdims=True)

    ri = lax.broadcasted_iota(jnp.int32, (PAIR, PAIR), 0)
    ci = lax.broadcasted_iota(jnp.int32, (PAIR, PAIR), 1)
    same = (ri >= GDN_CHUNK) == (ci >= GDN_CHUNK)
    eye_f32 = jnp.where(ri == ci, 1.0, 0.0)
    eye = eye_f32.astype(BF16)
    levels = int(math.log2(GDN_CHUNK))
    off_blocks = ([], [])
    for lvl in range(levels):
        half = 1 << lvl
        same_blk = (ri >> (lvl + 1)) == (ci >> (lvl + 1))
        r_hi = (ri & half) != 0
        c_hi = (ci & half) != 0
        off_blocks[0].append(same_blk & r_hi & jnp.logical_not(c_hi))
        off_blocks[1].append(same_blk & c_hi & jnp.logical_not(r_hi))

    gcols = [column(d * n_heads + h) for d in range(2)]
    bcols = [column(w2 + d * n_heads + h) for d in range(2)]
    tcols = [column(2 * w2 + d * n_heads + h) for d in range(2)]
    grows = [gr_ref[0, pl.ds(d * n_heads + h, 1), :] for d in range(2)]
    incls = (jnp.logical_and(same, ri >= ci), jnp.logical_and(same, ri <= ci))
    stricts = (jnp.logical_and(same, ri > ci), jnp.logical_and(same, ri < ci))

    for p in range(rb // PAIR):
        rows = slice(p * PAIR, (p + 1) * PAIR)
        q = q_ref[0, 0, rows, :]
        k = k_ref[0, 0, rows, :]
        v = v_ref[0, 0, rows, :]
        kbf = k.astype(BF16)
        kk = lax.dot_general(kbf, kbf, _NT, preferred_element_type=F32)
        qkm = lax.dot_general(q.astype(BF16), kbf, _NT, preferred_element_type=F32)
        for d in range(2):
            incl = incls[d]
            strict = stricts[d]
            gc = gcols[d][rows]
            bc = bcols[d][rows]
            tc = tcols[d][rows]
            gr = grows[d][:, rows]
            dec = jnp.where(incl, jnp.exp(jnp.where(incl, gc - gr, 0.0)), 0.0)
            a = jnp.where(strict, kk * bc * dec, 0.0)
            minv = eye_f32 - jnp.where(off_blocks[d][0], a, 0.0)
            for lvl in range(1, levels):
                lo = jnp.where(off_blocks[d][lvl], a, 0.0).astype(BF16)
                mb = minv.astype(BF16)
                t1 = jnp.dot(lo, mb, preferred_element_type=F32).astype(BF16)
                minv = minv - jnp.dot(mb, t1, preferred_element_type=F32)
            n = minv - eye_f32
            eg = jnp.exp(gc)
            vb = v * bc
            kbg = k * (bc * eg)
            rhs = jnp.concatenate([vb, kbg], axis=1).astype(BF16)
            uw = jnp.dot(n.astype(BF16), rhs, preferred_element_type=F32)
            u_ref[0, d, 0, rows, :] = vb + uw[:, :GDN_DV]
            w_ref[0, d, 0, rows, :] = (kbg + uw[:, GDN_DV:]).astype(BF16)
            qd_ref[0, d, 0, rows, :] = (q * eg).astype(BF16)
            qk_ref[0, d, 0, rows, :] = jnp.where(incl, qkm * dec, 0.0).astype(BF16)
            kd = (k * jnp.exp(tc - gc)).astype(BF16)
            kdt_ref[0, d, 0, rows, :] = lax.dot_general(
                eye, kd, _NT, preferred_element_type=F32).astype(BF16)


def _gdn_intra(qkv, gb, gr, n_heads):
    bsz, _, t, _ = qkv.shape
    rb = ROW_BLOCK
    nh = n_heads

    def qkv_spec(part):
        return pl.BlockSpec((1, 1, rb, LANES), lambda b, h, i: (b, part * nh + h, i, 0))

    out_spec = pl.BlockSpec((1, 2, 1, rb, LANES), lambda b, h, i: (b, 0, h, i, 0))
    shape = (bsz, 2, nh, t, LANES)
    return pl.pallas_call(
        functools.partial(_gdn_intra_kernel, n_heads=nh),
        grid=(bsz, nh, t // rb),
        in_specs=[qkv_spec(0), qkv_spec(1), qkv_spec(2),
                  pl.BlockSpec((1, rb, LANES), lambda b, h, i: (b, i, 0)),
                  pl.BlockSpec((1, gr.shape[1], rb), lambda b, h, i: (b, 0, i))],
        out_specs=[out_spec] * 5,
        out_shape=[jax.ShapeDtypeStruct(shape, F32)] + [jax.ShapeDtypeStruct(shape, BF16)] * 4,
        compiler_params=_params("parallel", "parallel", "parallel"),
        name="gdn_intra",
    )(qkv, qkv, qkv, gb, gr)


def _gdn_scan_kernel(gt_ref, uf, wf, qdf, qkf, kdtf, ub, wb, qdb, qkb, kdtb,
                     of_ref, ob_ref, s_ref, *, n_heads, hb, n_lat_blk, n_blk):
    b = pl.program_id(0)
    hblk = pl.program_id(1)
    i = pl.program_id(2)
    rb = uf.shape[3]
    n_ch = rb // GDN_CHUNK
    w2 = 2 * n_heads

    @pl.when(i == 0)
    def _():
        s_ref[...] = jnp.zeros(s_ref.shape, F32)

    blk_f = (i + n_lat_blk) % n_blk
    blk_b = n_blk - 1 - i
    zeros = jnp.zeros((GDN_CHUNK, GDN_DV), BF16)
    dirs = ((0, uf, wf, qdf, qkf, kdtf, of_ref, blk_f),
            (1, ub, wb, qdb, qkb, kdtb, ob_ref, blk_b))
    for c in range(n_ch):
        for d, u_r, w_r, qd_r, qk_r, kdt_r, o_r, blk in dirs:
            cc = c if d == 0 else n_ch - 1 - c
            r0 = cc * GDN_CHUNK
            rows = slice(r0, r0 + GDN_CHUNK)
            pair_rows = slice((cc // 2) * PAIR, (cc // 2 + 1) * PAIR)
            chunk = blk * n_ch + cc
            for hh in range(hb):
                head = hblk * hb + hh
                state = s_ref[d, hh]
                lhs1 = jnp.concatenate([w_r[0, 0, hh, rows, :], qd_r[0, 0, hh, rows, :]], axis=0)
                m1 = jnp.dot(lhs1, state.astype(BF16), preferred_element_type=F32)
                v_new = (u_r[0, 0, hh, rows, :] - m1[:GDN_CHUNK]).astype(BF16)
                v_ext = (jnp.concatenate([v_new, zeros], axis=0) if cc % 2 == 0
                         else jnp.concatenate([zeros, v_new], axis=0))
                lhs2 = jnp.concatenate([qk_r[0, 0, hh, rows, :], kdt_r[0, 0, hh, pair_rows, :]], axis=0)
                m2 = jnp.dot(lhs2, v_ext, preferred_element_type=F32)
                g_tot = gt_ref[(b * (n_blk * n_ch) + chunk) * w2 + d * n_heads + head]
                s_ref[d, hh] = state * g_tot + m2[GDN_CHUNK:]
                o_r[0, rows, hh * GDN_DV:(hh + 1) * GDN_DV] = m1[GDN_CHUNK:] + m2[:GDN_CHUNK]


def _gdn_scan(g_tot, u, w, qd, qk, kdt, n_heads, s):
    bsz, _, nh, t, _ = u.shape
    rb = ROW_BLOCK
    hb = _tile(nh, 4, 1)
    n_blk = t // rb
    n_lat_blk = s // rb

    def spec(d):
        if d == 0:
            return pl.BlockSpec((1, 1, hb, rb, LANES),
                                lambda b, h, i: (b, 0, h, (i + n_lat_blk) % n_blk, 0))
        return pl.BlockSpec((1, 1, hb, rb, LANES), lambda b, h, i: (b, 1, h, n_blk - 1 - i, 0))

    o_shape = jax.ShapeDtypeStruct((bsz, t, nh * GDN_DV), F32)
    return pl.pallas_call(
        functools.partial(_gdn_scan_kernel, n_heads=nh, hb=hb, n_lat_blk=n_lat_blk, n_blk=n_blk),
        grid=(bsz, nh // hb, n_blk),
        in_specs=[pl.BlockSpec(memory_space=pltpu.SMEM)] + [spec(0)] * 5 + [spec(1)] * 5,
        out_specs=[pl.BlockSpec((1, rb, hb * GDN_DV), lambda b, h, i: (b, (i + n_lat_blk) % n_blk, h)),
                   pl.BlockSpec((1, rb, hb * GDN_DV), lambda b, h, i: (b, n_blk - 1 - i, h))],
        out_shape=[o_shape, o_shape],
        scratch_shapes=[pltpu.VMEM((2, hb, GDN_DK, GDN_DV), F32)],
        compiler_params=_params("parallel", "parallel", "arbitrary"),
        name="gdn_scan",
    )(g_tot, u, w, qd, qk, kdt, u, w, qd, qk, kdt)


def _gdn_out_kernel(of_ref, ob_ref, z_ref, nw_ref, o_ref, *, n_heads):
    nw = nw_ref[...]
    for h in range(n_heads):
        cols = slice(h * GDN_DV, (h + 1) * GDN_DV)
        o = of_ref[0, :, cols] + ob_ref[0, :, cols]
        ms = jnp.mean(o * o, axis=-1, keepdims=True)
        o_ref[0, :, cols] = (o * lax.rsqrt(ms + 1e-6) * nw * _silu(z_ref[0, :, cols])).astype(BF16)


def _gdn_out(o_f, o_b, p3, z_col_blk, gdn_norm_w, n_heads, s):
    bsz = o_f.shape[0]
    width = n_heads * GDN_DV
    tm = ROW_BLOCK
    return pl.pallas_call(
        functools.partial(_gdn_out_kernel, n_heads=n_heads),
        grid=(bsz, s // tm),
        in_specs=[pl.BlockSpec((1, tm, width), lambda b, i: (b, i, 0)),
                  pl.BlockSpec((1, tm, width), lambda b, i: (b, i, 0)),
                  pl.BlockSpec((1, tm, width), lambda b, i: (b, i, z_col_blk)),
                  pl.BlockSpec((1, GDN_DV), lambda b, i: (0, 0))],
        out_specs=pl.BlockSpec((1, tm, width), lambda b, i: (b, i, 0)),
        out_shape=jax.ShapeDtypeStruct((bsz, s, width), BF16),
        compiler_params=_params("parallel", "parallel"),
        name="gdn_out",
    )(o_f, o_b, p3, gdn_norm_w.reshape(1, GDN_DV))


def _rope_kernel(q_ref, k_ref, v_ref, cos_ref, sin_ref, qo_ref, ko_ref, vo_ref):
    cos = cos_ref[...]
    sin = sin_ref[...]
    lane = lax.broadcasted_iota(jnp.int32, cos.shape, 1)
    first_half = (lane % (DIFF_DQK // 2)) < (DIFF_DQK // 4)
    quarter = DIFF_DQK // 4

    def rot(x):
        partner = jnp.where(first_half, pltpu.roll(x, LANES - quarter, axis=1),
                            pltpu.roll(x, quarter, axis=1))
        return x * cos + partner * sin

    qo_ref[0, 0] = (rot(q_ref[0]) * (DIFF_DQK ** -0.5)).astype(BF16)
    ko_ref[0, 0] = rot(k_ref[0]).astype(BF16)
    vo_ref[0, 0] = v_ref[0].astype(BF16)


def _rope(p3, q_blk, k_blk, v_blk, cos_t, sin_t, n_heads):
    bsz, t, _ = p3.shape
    rb = ROW_BLOCK

    def in_spec(off):
        return pl.BlockSpec((1, rb, LANES), lambda b, h, i: (b, i, off + h))

    tab = pl.BlockSpec((rb, LANES), lambda b, h, i: (i, 0))
    out = pl.BlockSpec((1, 1, rb, LANES), lambda b, h, i: (b, h, i, 0))
    shape = jax.ShapeDtypeStruct((bsz, n_heads, t, LANES), BF16)
    return pl.pallas_call(
        _rope_kernel,
        grid=(bsz, n_heads, t // rb),
        in_specs=[in_spec(q_blk), in_spec(k_blk), in_spec(v_blk), tab, tab],
        out_specs=[out, out, out],
        out_shape=[shape, shape, shape],
        compiler_params=_params("parallel", "parallel", "parallel"),
        name="diff_rope",
    )(p3, p3, p3, cos_t, sin_t)


def _attn_kernel(q_ref, k_ref, v_ref, lam_ref, nw_ref, o_ref, *, key_chunks):
    q = q_ref[0, 0]
    tq = q.shape[0]
    lane = lax.broadcasted_iota(jnp.int32, q.shape, 1)
    q_maps = (jnp.where(lane < DIFF_DQK, q, jnp.zeros_like(q)),
              jnp.where(lane >= DIFF_DQK, q, jnp.zeros_like(q)))
    lp = lam_ref[...]
    lam = (jnp.exp(jnp.sum(lp[0:1] * lp[1:2], axis=-1, keepdims=True))
           - jnp.exp(jnp.sum(lp[2:3] * lp[3:4], axis=-1, keepdims=True)) + LAM_INIT)
    m = [jnp.full((tq, 1), -jnp.inf, F32) for _ in range(2)]
    l = [jnp.zeros((tq, 1), F32) for _ in range(2)]
    acc = [jnp.zeros((tq, DIFF_DV), F32) for _ in range(2)]
    for start, size in key_chunks:
        k = k_ref[0, 0, start:start + size, :]
        v = v_ref[0, 0, start:start + size, :]
        for mi in range(2):
            s = lax.dot_general(q_maps[mi], k, _NT, preferred_element_type=F32)
            m_new = jnp.maximum(m[mi], jnp.max(s, axis=-1, keepdims=True))
            alpha = jnp.exp(m[mi] - m_new)
            e = jnp.exp(s - m_new)
            l[mi] = alpha * l[mi] + jnp.sum(e, axis=-1, keepdims=True)
            acc[mi] = alpha * acc[mi] + jnp.dot(e.astype(BF16), v, preferred_element_type=F32)
            m[mi] = m_new
    o = acc[0] / l[0] - lam * (acc[1] / l[1])
    ms = jnp.mean(o * o, axis=-1, keepdims=True)
    o_ref[0] = (o * lax.rsqrt(ms + 1e-5) * nw_ref[...] * (1.0 - LAM_INIT)).astype(BF16)


def _attention(qr, kr, vr, lam8, diff_norm_w, s):
    bsz, nh, t, _ = kr.shape
    tq = _tile(s, 512, ROW_BLOCK)
    kc = _tile(s, 1024, ROW_BLOCK)
    key_chunks = [(st, kc) for st in range(0, s, kc)] + [(s, t - s)]
    return pl.pallas_call(
        functools.partial(_attn_kernel, key_chunks=tuple(key_chunks)),
        grid=(bsz, nh, s // tq),
        in_specs=[pl.BlockSpec((1, 1, tq, LANES), lambda b, h, i: (b, h, i, 0)),
                  pl.BlockSpec((1, 1, t, LANES), lambda b, h, i: (b, h, 0, 0)),
                  pl.BlockSpec((1, 1, t, LANES), lambda b, h, i: (b, h, 0, 0)),
                  pl.BlockSpec((SUBLANES, LANES), lambda b, h, i: (0, 0)),
                  pl.BlockSpec((1, DIFF_DV), lambda b, h, i: (0, 0))],
        out_specs=pl.BlockSpec((1, tq, DIFF_DV), lambda b, h, i: (b, i, h)),
        out_shape=jax.ShapeDtypeStruct((bsz, s, nh * DIFF_DV), BF16),
        compiler_params=_params("parallel", "parallel", "arbitrary"),
        name="diff_attn",
    )(qr, kr, vr, lam8, diff_norm_w.reshape(1, DIFF_DV))


def _merge_kernel(ya_ref, yb_ref, wa_ref, wb_ref, ga_ref, gb_ref, o_ref):
    pa = jnp.dot(ya_ref[0], wa_ref[...], preferred_element_type=F32)
    pb = jnp.dot(yb_ref[0], wb_ref[...], preferred_element_type=F32)
    o_ref[0] = (jax.nn.sigmoid(ga_ref[0]) * pa + jax.nn.sigmoid(gb_ref[0]) * pb).astype(BF16)


def _merge(y_a, y_b, wa, wb, p3, ga_col, gb_col):
    bsz, s, ka = y_a.shape
    kb = y_b.shape[2]
    d = wa.shape[1]
    tm = _tile(s, 512, ROW_BLOCK)
    tn = _tile(math.gcd(d, ga_col, gb_col), 512, LANES)
    return pl.pallas_call(
        _merge_kernel,
        grid=(bsz, s // tm, d // tn),
        in_specs=[pl.BlockSpec((1, tm, ka), lambda b, i, j: (b, i, 0)),
                  pl.BlockSpec((1, tm, kb), lambda b, i, j: (b, i, 0)),
                  pl.BlockSpec((ka, tn), lambda b, i, j: (0, j)),
                  pl.BlockSpec((kb, tn), lambda b, i, j: (0, j)),
                  pl.BlockSpec((1, tm, tn), lambda b, i, j: (b, i, ga_col // tn + j)),
                  pl.BlockSpec((1, tm, tn), lambda b, i, j: (b, i, gb_col // tn + j))],
        out_specs=pl.BlockSpec((1, tm, tn), lambda b, i, j: (b, i, j)),
        out_shape=jax.ShapeDtypeStruct((bsz, s, d), BF16),
        compiler_params=_params("parallel", "parallel", "arbitrary"),
        name="merge",
    )(y_a, y_b, wa, wb, p3, p3)


def _layernorm_rows(ref, g, bvec, emit, rows_per_step=32):
    tm = ref.shape[0]

    def body(r, carry):
        rows = pl.ds(pl.multiple_of(r * rows_per_step, rows_per_step), rows_per_step)
        x = ref[rows, :]
        mu = jnp.mean(x, axis=-1, keepdims=True)
        xc = x - mu
        var = jnp.mean(xc * xc, axis=-1, keepdims=True)
        y = xc * lax.rsqrt(var + LN_EPS) * g + bvec
        ref[rows, :] = y
        emit(rows, y)
        return carry

    lax.fori_loop(0, tm // rows_per_step, body, 0)


def _proj_ln_kernel(a_ref, w_ref, x_ref, gt_ref, g_ref, b_ref, sc_ref, sh_ref,
                    o_ref, h_ref, *, alpha, n_j, tn, rows_per_batch):
    i = pl.program_id(0)
    j = pl.program_id(1)
    tm = a_ref.shape[0]
    b = (i * tm) // rows_per_batch
    mix = jnp.dot(a_ref[...], w_ref[...], preferred_element_type=F32)
    r = alpha * x_ref[...] + gt_ref[pl.ds(b, 1), :] * mix
    for jj in range(n_j):
        @pl.when(j == jj)
        def _(jj=jj):
            o_ref[:, jj * tn:(jj + 1) * tn] = r

    @pl.when(j == n_j - 1)
    def _():
        if h_ref is None:
            _layernorm_rows(o_ref, g_ref[...], b_ref[...], lambda rows, y: None)
        else:
            sc = 1.0 + sc_ref[pl.ds(b, 1), :]
            sh = sh_ref[pl.ds(b, 1), :]

            def emit(rows, y):
                h_ref[rows, :] = (y * sc + sh).astype(BF16)

            _layernorm_rows(o_ref, g_ref[...], b_ref[...], emit)


def _proj_ln_kernel_noh(a_ref, w_ref, x_ref, gt_ref, g_ref, b_ref, o_ref, **kw):
    _proj_ln_kernel(a_ref, w_ref, x_ref, gt_ref, g_ref, b_ref, None, None, o_ref, None, **kw)


def _proj_ln(a, w, x, mod, gt_chunk, ln_g, ln_b, rows_per_batch, *, alpha, tm, tn, name,
             next_mod_chunks=None):
    m, k = a.shape
    d = w.shape[1]
    n_j = d // tn
    kw = dict(alpha=alpha, n_j=n_j, tn=tn, rows_per_batch=rows_per_batch)
    in_specs = [pl.BlockSpec((tm, k), lambda i, j: (i, 0)),
                pl.BlockSpec((k, tn), lambda i, j: (0, j)),
                pl.BlockSpec((tm, tn), lambda i, j: (i, j)),
                pl.BlockSpec((SUBLANES, tn), lambda i, j: (0, gt_chunk * n_j + j)),
                pl.BlockSpec((1, d), lambda i, j: (0, 0)),
                pl.BlockSpec((1, d), lambda i, j: (0, 0))]
    args = [a, w, x, mod, ln_g.reshape(1, d), ln_b.reshape(1, d)]
    row_spec = pl.BlockSpec((tm, d), lambda i, j: (i, 0))
    if next_mod_chunks is None:
        kernel = functools.partial(_proj_ln_kernel_noh, **kw)
        out_specs = row_spec
        out_shape = jax.ShapeDtypeStruct((m, d), F32)
    else:
        sh_chunk, sc_chunk = next_mod_chunks
        kernel = functools.partial(_proj_ln_kernel, **kw)
        in_specs += [pl.BlockSpec((SUBLANES, d), lambda i, j: (0, sc_chunk)),
                     pl.BlockSpec((SUBLANES, d), lambda i, j: (0, sh_chunk))]
        args += [mod, mod]
        out_specs = [row_spec, row_spec]
        out_shape = [jax.ShapeDtypeStruct((m, d), F32), jax.ShapeDtypeStruct((m, d), BF16)]
    return pl.pallas_call(
        kernel,
        grid=(m // tm, n_j),
        in_specs=in_specs,
        out_specs=out_specs,
        out_shape=out_shape,
        compiler_params=_params("parallel", "arbitrary"),
        name=name,
    )(*args)


def _ffn_up_kernel(h_ref, wg_ref, wu_ref, o_ref):
    h = h_ref[...]
    g = jnp.dot(h, wg_ref[...], preferred_element_type=F32)
    u = jnp.dot(h, wu_ref[...], preferred_element_type=F32)
    o_ref[...] = (_silu(g) * u).astype(BF16)


def _ffn_up(h, wg, wu):
    m, d = h.shape
    f = wg.shape[1]
    tm = _tile(m, 1024, ROW_BLOCK)
    tn = _tile(f, 512, LANES)
    return pl.pallas_call(
        _ffn_up_kernel,
        grid=(m // tm, f // tn),
        in_specs=[pl.BlockSpec((tm, d), lambda i, j: (i, 0)),
                  pl.BlockSpec((d, tn), lambda i, j: (0, j)),
                  pl.BlockSpec((d, tn), lambda i, j: (0, j))],
        out_specs=pl.BlockSpec((tm, tn), lambda i, j: (i, j)),
        out_shape=jax.ShapeDtypeStruct((m, f), BF16),
        compiler_params=_params("parallel", "arbitrary"),
        name="ffn_up",
    )(h, wg, wu)


def _rope_tables(s, c):
    quarter = DIFF_DQK // 4
    inv_freq = ROPE_BASE ** (-jnp.arange(quarter, dtype=F32) / quarter)
    pos = jnp.arange(s, dtype=jnp.int32)
    lane = jnp.arange(LANES)
    use_col = (lane % DIFF_DQK) >= (DIFF_DQK // 2)
    p = jnp.where(use_col[None, :], (pos % GRID_W)[:, None], (pos // GRID_W)[:, None]).astype(F32)
    ang = p * inv_freq[lane % quarter][None, :]
    sign = jnp.where((lane % (2 * quarter)) < quarter, -1.0, 1.0)[None, :]
    cos = jnp.concatenate([jnp.cos(ang), jnp.ones((c, LANES), F32)], axis=0)
    sin = jnp.concatenate([jnp.sin(ang) * sign, jnp.zeros((c, LANES), F32)], axis=0)
    return cos, sin


def kernel(x, c, ctx, c_ctx, w_ada, b_ada, w_in, conv_w, a_log, dt_bias, gdn_norm_w, lam_q1, lam_k1,
           lam_q2, lam_k2, diff_norm_w, w_proj_a, w_proj_b, w_out, ln1_g, ln1_b, w_gate, w_up, w_down,
           ln2_g, ln2_b):
    bsz, s, d = x.shape
    cl = ctx.shape[1]
    t = s + cl
    nh = d // HEAD_W
    qk_w = nh * GDN_DK
    v_w = nh * GDN_DV
    dqk_w = nh * 2 * DIFF_DQK
    dv_w = nh * DIFF_DV
    alpha = (2.0 * w_ada.shape[0]) ** 0.25
    assert w_ada.shape[0] == 1 and s % ROW_BLOCK == 0 and cl % ROW_BLOCK == 0 and 8 * nh <= LANES

    c8 = jnp.zeros((SUBLANES, d), F32).at[:bsz].set(c).at[bsz].set(c_ctx)
    mod = _ada(c8, w_ada[0], b_ada[0])

    h_all = _modulate(x, ctx, mod, bsz)

    w_l = w_in[0]
    gqkv_end = 2 * qk_w + v_w
    ab_start = gqkv_end + v_w
    ab_end = ab_start + 4 * nh
    w_main = jnp.concatenate([w_l[:, :ab_start], w_l[:, ab_end:]], axis=1).astype(BF16)
    w_ab = jnp.pad(w_l[:, ab_start:ab_end], ((0, 0), (0, LANES - 4 * nh))).astype(BF16)
    h2d = h_all.reshape(bsz * t, d)
    n_main = w_main.shape[1]
    p3 = _matmul(h2d, w_main, tm=_tile(bsz * t, 512, ROW_BLOCK), tn=_tile(n_main, 1024, LANES),
                 name="in_proj").reshape(bsz, t, n_main)
    p_ab = _matmul(h2d, w_ab, tm=_tile(bsz * t, 512, ROW_BLOCK), tn=LANES,
                   name="in_proj_ab").reshape(bsz, t, LANES)
    z_col = gqkv_end
    dq_col = ab_start
    dk_col = dq_col + dqk_w
    dv_col = dk_col + dqk_w
    ga_col = dv_col + dv_w
    gb_col = ga_col + d

    w2 = 2 * nh
    alog128 = jnp.pad(a_log[0].reshape(1, w2), ((0, 0), (0, LANES - w2)))
    dtb128 = jnp.pad(dt_bias[0].reshape(1, w2), ((0, 0), (0, LANES - w2)))
    gb = _gates(p_ab, alog128, dtb128, w2)
    gr = jnp.swapaxes(gb[:, :, :w2], 1, 2)
    g_tot = gb[:, ::GDN_CHUNK, 3 * w2:4 * w2].reshape(-1)
    qkv = _gdn_conv(p3, conv_w[0], nh, s)
    u, w_, qd, qk, kdt = _gdn_intra(qkv, gb, gr, nh)
    o_f, o_b = _gdn_scan(g_tot, u, w_, qd, qk, kdt, nh, s)
    y_a = _gdn_out(o_f, o_b, p3, z_col // v_w, gdn_norm_w[0], nh, s)

    cos_t, sin_t = _rope_tables(s, cl)
    qr, kr, vr = _rope(p3, dq_col // LANES, dk_col // LANES, dv_col // LANES, cos_t, sin_t, nh)
    lam8 = jnp.zeros((SUBLANES, LANES), F32)
    for r, vec in enumerate((lam_q1, lam_k1, lam_q2, lam_k2)):
        lam8 = lam8.at[r, :DIFF_DQK].set(vec[0])
    y_b = _attention(qr, kr, vr, lam8, diff_norm_w[0], s)

    ymg = _merge(y_a, y_b, w_proj_a[0].astype(BF16), w_proj_b[0].astype(BF16), p3, ga_col, gb_col)
    m = bsz * s
    x1, h2 = _proj_ln(ymg.reshape(m, d), w_out[0].astype(BF16), x.reshape(m, d), mod, 2,
                      ln1_g[0], ln1_b[0], s, alpha=alpha, tm=_tile(s, 512, ROW_BLOCK),
                      tn=_tile(d, 512, LANES), name="out_proj_ln", next_mod_chunks=(3, 4))

    hff = _ffn_up(h2, w_gate[0].astype(BF16), w_up[0].astype(BF16))
    out = _proj_ln(hff, w_down[0].astype(BF16), x1, mod, 5, ln2_g[0], ln2_b[0], s, alpha=alpha,
                   tm=_tile(s, 512, ROW_BLOCK), tn=_tile(d, 256, LANES), name="ffn_down_ln")
    return out.reshape(bsz, s, d)
```

```python
import functools
import math

import jax
import jax.numpy as jnp
from jax import lax
from jax.experimental import pallas as pl
from jax.experimental.pallas import tpu as pltpu

F32 = jnp.float32
BF16 = jnp.bfloat16

GRID_W = 64
HEAD_W = 256
GDN_DK = 128
GDN_DV = 128
GDN_CHUNK = 64
DIFF_DQK = 64
DIFF_DV = 128
ROPE_BASE = 10000.0
LN_EPS = 1e-6
LAM_INIT = 0.8 - 0.6 * math.exp(-0.3 * 0)

LANES = 128
SUBLANES = 8
PAIR = 2 * GDN_CHUNK
ROW_BLOCK = 256
VMEM_LIMIT = 56 * 1024 * 1024

_NT = (((1,), (1,)), ((), ()))


def _params(*sem):
    return pltpu.CompilerParams(dimension_semantics=sem, vmem_limit_bytes=VMEM_LIMIT)


def _tile(n, target, quantum):
    best = None
    t = quantum
    while t <= min(n, target):
        if n % t == 0:
            best = t
        t += quantum
    assert best is not None, (n, target, quantum)
    return best


def _silu(x):
    return x * jax.nn.sigmoid(x)


def _ada_kernel(c_ref, w_ref, b_ref, o_ref):
    c = _silu(c_ref[...])
    o_ref[...] = jnp.dot(c.astype(BF16), w_ref[...].astype(BF16),
                         preferred_element_type=F32) + b_ref[...]


def _ada(c8, w_ada, b_ada):
    d, n = w_ada.shape
    tn = _tile(n, 512, LANES)
    return pl.pallas_call(
        _ada_kernel,
        grid=(n // tn,),
        in_specs=[pl.BlockSpec((SUBLANES, d), lambda j: (0, 0)),
                  pl.BlockSpec((d, tn), lambda j: (0, j)),
                  pl.BlockSpec((1, tn), lambda j: (0, j))],
        out_specs=pl.BlockSpec((SUBLANES, tn), lambda j: (0, j)),
        out_shape=jax.ShapeDtypeStruct((SUBLANES, n), F32),
        compiler_params=_params("parallel"),
        name="ada",
    )(c8, w_ada, b_ada.reshape(1, n))


def _modulate_kernel(x_ref, ctx_ref, sh_ref, sc_ref, o_ref, *, n_lat, ctx_row):
    b = pl.program_id(0)
    i = pl.program_id(1)

    @pl.when(i < n_lat)
    def _():
        sc = sc_ref[pl.ds(b, 1), :]
        sh = sh_ref[pl.ds(b, 1), :]
        o_ref[0] = (x_ref[0] * (1.0 + sc) + sh).astype(BF16)

    @pl.when(i >= n_lat)
    def _():
        sc = sc_ref[ctx_row:ctx_row + 1, :]
        sh = sh_ref[ctx_row:ctx_row + 1, :]
        o_ref[0] = (ctx_ref[0] * (1.0 + sc) + sh).astype(BF16)


def _modulate(x, ctx, mod, ctx_row):
    bsz, s, d = x.shape
    c = ctx.shape[1]
    t = s + c
    rb = ROW_BLOCK
    n_lat = s // rb
    return pl.pallas_call(
        functools.partial(_modulate_kernel, n_lat=n_lat, ctx_row=ctx_row),
        grid=(bsz, t // rb),
        in_specs=[pl.BlockSpec((1, rb, d), lambda b, i: (b, jnp.minimum(i, n_lat - 1), 0)),
                  pl.BlockSpec((1, rb, d), lambda b, i: (b, jnp.maximum(i - n_lat, 0), 0)),
                  pl.BlockSpec((SUBLANES, d), lambda b, i: (0, 0)),
                  pl.BlockSpec((SUBLANES, d), lambda b, i: (0, 1))],
        out_specs=pl.BlockSpec((1, rb, d), lambda b, i: (b, i, 0)),
        out_shape=jax.ShapeDtypeStruct((bsz, t, d), BF16),
        compiler_params=_params("parallel", "arbitrary"),
        name="modulate",
    )(x, ctx, mod, mod)


CAST_ROWS = 256


def _in_proj_kernel(*refs, shift):
    if shift:
        a_ref, w_ref, wn_ref, o_ref, wb_ref = refs
    else:
        a_ref, w_ref, o_ref, wb_ref = refs
    k, tn = w_ref.shape

    @pl.when(pl.program_id(1) == 0)
    def _():
        def body(r, carry):
            rows = pl.ds(pl.multiple_of(r * CAST_ROWS, CAST_ROWS), CAST_ROWS)
            if shift:
                full = jnp.concatenate([w_ref[rows, :], wn_ref[rows, :]], axis=1)
                w = pltpu.roll(full, tn + LANES - shift, axis=1)[:, :tn]
            else:
                w = w_ref[rows, :]
            wb_ref[rows, :] = w.astype(BF16)
            return carry

        lax.fori_loop(0, k // CAST_ROWS, body, 0)

    o_ref[...] = jnp.dot(a_ref[...], wb_ref[...], preferred_element_type=F32)


def _in_proj(a, w, col0, n, *, tm, tn, name):
    m, k = a.shape
    shift = col0 % LANES
    base = col0 - shift
    assert base % tn == 0 and n % tn == 0 and k % CAST_ROWS == 0
    in_specs = [pl.BlockSpec((tm, k), lambda j, i: (i, 0)),
                pl.BlockSpec((k, tn), lambda j, i: (0, base // tn + j))]
    args = [a, w]
    if shift:
        in_specs.append(pl.BlockSpec((k, LANES), lambda j, i: (0, (base + (j + 1) * tn) // LANES)))
        args.append(w)
    return pl.pallas_call(
        functools.partial(_in_proj_kernel, shift=shift),
        grid=(n // tn, m // tm),
        in_specs=in_specs,
        out_specs=pl.BlockSpec((tm, tn), lambda j, i: (i, j)),
        out_shape=jax.ShapeDtypeStruct((m, n), F32),
        scratch_shapes=[pltpu.VMEM((k, tn), BF16)],
        compiler_params=_params("parallel", "arbitrary"),
        name=name,
    )(*args)


def _gates_kernel(p_ref, alog_ref, dtb_ref, o_ref, *, w2):
    p = p_ref[0]
    rb = p.shape[0]
    lane = lax.broadcasted_iota(jnp.int32, p.shape, 1)
    row = lax.broadcasted_iota(jnp.int32, p.shape, 0) % GDN_CHUNK
    xa = p + dtb_ref[...]
    softplus = jnp.maximum(xa, 0.0) + jnp.log(1.0 + jnp.exp(-jnp.abs(xa)))
    g = jnp.where(lane < w2, -jnp.exp(alog_ref[...]) * softplus, 0.0)
    pre = g
    suf = g
    step = 1
    while step < GDN_CHUNK:
        pre = pre + jnp.where(row >= step, pltpu.roll(pre, step, axis=0), 0.0)
        suf = suf + jnp.where(row < GDN_CHUNK - step, pltpu.roll(suf, rb - step, axis=0), 0.0)
        step *= 2
    tot = pre + suf - g
    cum = jnp.where(lane < w2 // 2, pre, suf)
    beta = jax.nn.sigmoid(p)
    out = jnp.where(lane < w2, cum,
                    jnp.where(lane < 2 * w2, beta,
                              jnp.where(lane < 3 * w2, pltpu.roll(tot, 2 * w2, axis=1),
                                        pltpu.roll(jnp.exp(tot), 3 * w2, axis=1))))
    o_ref[0] = jnp.where(lane < 4 * w2, out, 0.0)


def _gates(p_ab, alog128, dtb128, w2):
    bsz, t, _ = p_ab.shape
    rb = ROW_BLOCK
    return pl.pallas_call(
        functools.partial(_gates_kernel, w2=w2),
        grid=(bsz, t // rb),
        in_specs=[pl.BlockSpec((1, rb, LANES), lambda b, i: (b, i, 0)),
                  pl.BlockSpec((1, LANES), lambda b, i: (0, 0)),
                  pl.BlockSpec((1, LANES), lambda b, i: (0, 0))],
        out_specs=pl.BlockSpec((1, rb, LANES), lambda b, i: (b, i, 0)),
        out_shape=jax.ShapeDtypeStruct((bsz, t, LANES), F32),
        compiler_params=_params("parallel", "parallel"),
        name="gdn_gates",
    )(p_ab, alog128, dtb128)


def _conv_kernel(x_ref, cw_ref, o_ref, xp_ref, *, n_heads, n_lat, n_blk, taps):
    j = pl.program_id(1)
    t = x_ref.shape[1]
    rb = ROW_BLOCK
    halo = SUBLANES
    pad = taps // 2
    win_rows = rb + 2 * halo
    xp_ref[0:halo, :] = jnp.zeros((halo, LANES), F32)
    xp_ref[halo + t:2 * halo + t, :] = jnp.zeros((halo, LANES), F32)
    xp_ref[halo:halo + t, :] = x_ref[0]
    w = cw_ref[...]
    ridx = lax.broadcasted_iota(jnp.int32, (win_rows, LANES), 0)
    is_qk = j < 2 * n_heads
    q_scale = jnp.where(j < n_heads, GDN_DK ** -0.5, 1.0)

    def body(r, carry):
        base = pl.multiple_of(r * rb, rb)
        win = xp_ref[pl.ds(base, win_rows), :]
        top_ok = jnp.logical_and(r != 0, r != n_lat)
        bot_ok = jnp.logical_and(r != n_lat - 1, r != n_blk - 1)
        kill = jnp.logical_or(jnp.logical_and(ridx < halo, jnp.logical_not(top_ok)),
                              jnp.logical_and(ridx >= halo + rb, jnp.logical_not(bot_ok)))
        win = jnp.where(kill, 0.0, win)
        acc = jnp.zeros((win_rows, LANES), F32)
        for tap in range(taps):
            shift = (pad - tap) % win_rows
            shifted = win if shift == 0 else pltpu.roll(win, shift, axis=0)
            acc = acc + shifted * w[tap:tap + 1, :]
        y = _silu(acc[halo:halo + rb, :])
        ss = jnp.sum(y * y, axis=-1, keepdims=True)
        scale = jnp.where(is_qk, lax.rsqrt(ss + 1e-6) * q_scale, 1.0)
        o_ref[0, 0, pl.ds(base, rb), :] = y * scale
        return carry

    lax.fori_loop(0, n_blk, body, 0)


def _gdn_conv(p3, conv_w, n_heads, s):
    bsz, t, _ = p3.shape
    taps = conv_w.shape[0]
    n_cols = 3 * n_heads
    return pl.pallas_call(
        functools.partial(_conv_kernel, n_heads=n_heads, n_lat=s // ROW_BLOCK,
                          n_blk=t // ROW_BLOCK, taps=taps),
        grid=(bsz, n_cols),
        in_specs=[pl.BlockSpec((1, t, LANES), lambda b, j: (b, 0, j)),
                  pl.BlockSpec((taps, LANES), lambda b, j: (0, j))],
        out_specs=pl.BlockSpec((1, 1, t, LANES), lambda b, j: (b, j, 0, 0)),
        out_shape=jax.ShapeDtypeStruct((bsz, n_cols, t, LANES), F32),
        scratch_shapes=[pltpu.VMEM((t + 2 * SUBLANES, LANES), F32)],
        compiler_params=_params("parallel", "parallel"),
        name="gdn_conv",
    )(p3, conv_w)


def _gdn_intra_kernel(q_ref, k_ref, v_ref, gb_ref, gr_ref,
                      u_ref, w_ref, qd_ref, qk_ref, kdt_ref, *, n_heads, hb):
    hblk = pl.program_id(1)
    w2 = 2 * n_heads
    rb = q_ref.shape[2]
    gbv = gb_ref[0]
    lane = lax.broadcasted_iota(jnp.int32, gbv.shape, 1)

    def column(idx):
        return jnp.sum(jnp.where(lane == idx, gbv, 0.0), axis=-1, keepdims=True)

    ri = lax.broadcasted_iota(jnp.int32, (PAIR, PAIR), 0)
    ci = lax.broadcasted_iota(jnp.int32, (PAIR, PAIR), 1)
    same = (ri >= GDN_CHUNK) == (ci >= GDN_CHUNK)
    eye_f32 = jnp.where(ri == ci, 1.0, 0.0)
    eye = eye_f32.astype(BF16)
    levels = int(math.log2(GDN_CHUNK))
    off_blocks = ([], [])
    for lvl in range(levels):
        half = 1 << lvl
        same_blk = (ri >> (lvl + 1)) == (ci >> (lvl + 1))
        r_hi = (ri & half) != 0
        c_hi = (ci & half) != 0
        off_blocks[0].append(same_blk & r_hi & jnp.logical_not(c_hi))
        off_blocks[1].append(same_blk & c_hi & jnp.logical_not(r_hi))

    incls = (jnp.logical_and(same, ri >= ci), jnp.logical_and(same, ri <= ci))
    stricts = (jnp.logical_and(same, ri > ci), jnp.logical_and(same, ri < ci))

    probs = []
    for hh in range(hb):
        head = hblk * hb + hh
        gcols = [column(d * n_heads + head) for d in range(2)]
        bcols = [column(w2 + d * n_heads + head) for d in range(2)]
        tcols = [column(2 * w2 + d * n_heads + head) for d in range(2)]
        grows = [gr_ref[0, pl.ds(d * n_heads + head, 1), :] for d in range(2)]
        for p in range(rb // PAIR):
            rows = slice(p * PAIR, (p + 1) * PAIR)
            q = q_ref[0, hh, rows, :]
            k = k_ref[0, hh, rows, :]
            kbf = k.astype(BF16)
            kk = lax.dot_general(kbf, kbf, _NT, preferred_element_type=F32)
            qkm = lax.dot_general(q.astype(BF16), kbf, _NT, preferred_element_type=F32)
            for d in range(2):
                gc = gcols[d][rows]
                bc = bcols[d][rows]
                dec = jnp.where(incls[d], jnp.exp(jnp.where(incls[d], gc - grows[d][:, rows], 0.0)), 0.0)
                a = jnp.where(stricts[d], kk * bc * dec, 0.0)
                qk_ref[0, d, hh, rows, :] = jnp.where(incls[d], qkm * dec, 0.0).astype(BF16)
                probs.append(dict(hh=hh, rows=rows, d=d, a=a, gc=gc, bc=bc, tc=tcols[d][rows]))

    minv = [eye_f32 - jnp.where(off_blocks[pr["d"]][0], pr["a"], 0.0) for pr in probs]
    for lvl in range(1, levels):
        mb = [m.astype(BF16) for m in minv]
        t1 = [jnp.dot(jnp.where(off_blocks[pr["d"]][lvl], pr["a"], 0.0).astype(BF16), mb[i],
                      preferred_element_type=F32).astype(BF16) for i, pr in enumerate(probs)]
        minv = [minv[i] - jnp.dot(mb[i], t1[i], preferred_element_type=F32) for i in range(len(probs))]

    for i, pr in enumerate(probs):
        hh, rows, d, gc, bc = pr["hh"], pr["rows"], pr["d"], pr["gc"], pr["bc"]
        q = q_ref[0, hh, rows, :]
        k = k_ref[0, hh, rows, :]
        v = v_ref[0, hh, rows, :]
        eg = jnp.exp(gc)
        vb = v * bc
        kbg = k * (bc * eg)
        rhs = jnp.concatenate([vb, kbg], axis=1).astype(BF16)
        uw = jnp.dot((minv[i] - eye_f32).astype(BF16), rhs, preferred_element_type=F32)
        u_ref[0, d, hh, rows, :] = vb + uw[:, :GDN_DV]
        w_ref[0, d, hh, rows, :] = (kbg + uw[:, GDN_DV:]).astype(BF16)
        qd_ref[0, d, hh, rows, :] = (q * eg).astype(BF16)
        kd = (k * jnp.exp(pr["tc"] - gc)).astype(BF16)
        kdt_ref[0, d, hh, rows, :] = lax.dot_general(
            eye, kd, _NT, preferred_element_type=F32).astype(BF16)


def _gdn_intra(qkv, gb, gr, n_heads):
    bsz, _, t, _ = qkv.shape
    rb = ROW_BLOCK
    nh = n_heads
    hb = _tile(nh, 2, 1)
    nhb = nh // hb

    def qkv_spec(part):
        return pl.BlockSpec((1, hb, rb, LANES), lambda b, h, i: (b, part * nhb + h, i, 0))

    out_spec = pl.BlockSpec((1, 2, hb, rb, LANES), lambda b, h, i: (b, 0, h, i, 0))
    shape = (bsz, 2, nh, t, LANES)
    return pl.pallas_call(
        functools.partial(_gdn_intra_kernel, n_heads=nh, hb=hb),
        grid=(bsz, nhb, t // rb),
        in_specs=[qkv_spec(0), qkv_spec(1), qkv_spec(2),
                  pl.BlockSpec((1, rb, LANES), lambda b, h, i: (b, i, 0)),
                  pl.BlockSpec((1, gr.shape[1], rb), lambda b, h, i: (b, 0, i))],
        out_specs=[out_spec] * 5,
        out_shape=[jax.ShapeDtypeStruct(shape, F32)] + [jax.ShapeDtypeStruct(shape, BF16)] * 4,
        compiler_params=_params("parallel", "parallel", "parallel"),
        name="gdn_intra",
    )(qkv, qkv, qkv, gb, gr)


def _gdn_scan_kernel(gt_ref, uf, wf, qdf, qkf, kdtf, ub, wb, qdb, qkb, kdtb,
                     of_ref, ob_ref, s_ref, *, n_heads, hb, n_lat_blk, n_blk):
    b = pl.program_id(0)
    hblk = pl.program_id(1)
    i = pl.program_id(2)
    rb = uf.shape[3]
    n_ch = rb // GDN_CHUNK
    w2 = 2 * n_heads

    @pl.when(i == 0)
    def _():
        s_ref[...] = jnp.zeros(s_ref.shape, F32)

    blk_f = (i + n_lat_blk) % n_blk
    blk_b = n_blk - 1 - i
    zeros = jnp.zeros((GDN_CHUNK, GDN_DV), BF16)
    dirs = ((0, uf, wf, qdf, qkf, kdtf, of_ref, blk_f),
            (1, ub, wb, qdb, qkb, kdtb, ob_ref, blk_b))
    for c in range(n_ch):
        for d, u_r, w_r, qd_r, qk_r, kdt_r, o_r, blk in dirs:
            cc = c if d == 0 else n_ch - 1 - c
            r0 = cc * GDN_CHUNK
            rows = slice(r0, r0 + GDN_CHUNK)
            pair_rows = slice((cc // 2) * PAIR, (cc // 2 + 1) * PAIR)
            chunk = blk * n_ch + cc
            for hh in range(hb):
                head = hblk * hb + hh
                state = s_ref[d, hh]
                lhs1 = jnp.concatenate([w_r[0, 0, hh, rows, :], qd_r[0, 0, hh, rows, :]], axis=0)
                m1 = jnp.dot(lhs1, state.astype(BF16), preferred_element_type=F32)
                v_new = (u_r[0, 0, hh, rows, :] - m1[:GDN_CHUNK]).astype(BF16)
                v_ext = (jnp.concatenate([v_new, zeros], axis=0) if cc % 2 == 0
                         else jnp.concatenate([zeros, v_new], axis=0))
                lhs2 = jnp.concatenate([qk_r[0, 0, hh, rows, :], kdt_r[0, 0, hh, pair_rows, :]], axis=0)
                m2 = jnp.dot(lhs2, v_ext, preferred_element_type=F32)
                g_tot = gt_ref[(b * (n_blk * n_ch) + chunk) * w2 + d * n_heads + head]
                s_ref[d, hh] = state * g_tot + m2[GDN_CHUNK:]
                o_r[0, rows, hh * GDN_DV:(hh + 1) * GDN_DV] = m1[GDN_CHUNK:] + m2[:GDN_CHUNK]


def _gdn_scan(g_tot, u, w, qd, qk, kdt, n_heads, s):
    bsz, _, nh, t, _ = u.shape
    rb = ROW_BLOCK
    hb = _tile(nh, 4, 1)
    n_blk = t // rb
    n_lat_blk = s // rb

    def spec(d):
        if d == 0:
            return pl.BlockSpec((1, 1, hb, rb, LANES),
                                lambda b, h, i: (b, 0, h, (i + n_lat_blk) % n_blk, 0))
        return pl.BlockSpec((1, 1, hb, rb, LANES), lambda b, h, i: (b, 1, h, n_blk - 1 - i, 0))

    o_shape = jax.ShapeDtypeStruct((bsz, t, nh * GDN_DV), F32)
    return pl.pallas_call(
        functools.partial(_gdn_scan_kernel, n_heads=nh, hb=hb, n_lat_blk=n_lat_blk, n_blk=n_blk),
        grid=(bsz, nh // hb, n_blk),
        in_specs=[pl.BlockSpec(memory_space=pltpu.SMEM)] + [spec(0)] * 5 + [spec(1)] * 5,
        out_specs=[pl.BlockSpec((1, rb, hb * GDN_DV), lambda b, h, i: (b, (i + n_lat_blk) % n_blk, h)),
                   pl.BlockSpec((1, rb, hb * GDN_DV), lambda b, h, i: (b, n_blk - 1 - i, h))],
        out_shape=[o_shape, o_shape],
        scratch_shapes=[pltpu.VMEM((2, hb, GDN_DK, GDN_DV), F32)],
        compiler_params=_params("parallel", "parallel", "arbitrary"),
        name="gdn_scan",
    )(g_tot, u, w, qd, qk, kdt, u, w, qd, qk, kdt)


def _gdn_out_kernel(of_ref, ob_ref, z_ref, nw_ref, o_ref, *, n_heads):
    nw = nw_ref[...]
    for h in range(n_heads):
        cols = slice(h * GDN_DV, (h + 1) * GDN_DV)
        o = of_ref[0, :, cols] + ob_ref[0, :, cols]
        ms = jnp.mean(o * o, axis=-1, keepdims=True)
        o_ref[0, :, cols] = (o * lax.rsqrt(ms + 1e-6) * nw * _silu(z_ref[0, :, cols])).astype(BF16)


def _gdn_out(o_f, o_b, p3, z_col_blk, gdn_norm_w, n_heads, s):
    bsz = o_f.shape[0]
    width = n_heads * GDN_DV
    tm = ROW_BLOCK
    return pl.pallas_call(
        functools.partial(_gdn_out_kernel, n_heads=n_heads),
        grid=(bsz, s // tm),
        in_specs=[pl.BlockSpec((1, tm, width), lambda b, i: (b, i, 0)),
                  pl.BlockSpec((1, tm, width), lambda b, i: (b, i, 0)),
                  pl.BlockSpec((1, tm, width), lambda b, i: (b, i, z_col_blk)),
                  pl.BlockSpec((1, GDN_DV), lambda b, i: (0, 0))],
        out_specs=pl.BlockSpec((1, tm, width), lambda b, i: (b, i, 0)),
        out_shape=jax.ShapeDtypeStruct((bsz, s, width), BF16),
        compiler_params=_params("parallel", "parallel"),
        name="gdn_out",
    )(o_f, o_b, p3, gdn_norm_w.reshape(1, GDN_DV))


def _rope_kernel(q_ref, k_ref, v_ref, cos_ref, sin_ref, qo_ref, ko_ref, vo_ref, *, n_heads):
    cos = cos_ref[...]
    sin = sin_ref[...]
    lane = lax.broadcasted_iota(jnp.int32, cos.shape, 1)
    first_half = (lane % (DIFF_DQK // 2)) < (DIFF_DQK // 4)
    quarter = DIFF_DQK // 4

    def rot(x):
        partner = jnp.where(first_half, pltpu.roll(x, LANES - quarter, axis=1),
                            pltpu.roll(x, quarter, axis=1))
        return x * cos + partner * sin

    for h in range(n_heads):
        cols = slice(h * LANES, (h + 1) * LANES)
        qo_ref[0, h] = (rot(q_ref[0, :, cols]) * (DIFF_DQK ** -0.5)).astype(BF16)
        ko_ref[0, h] = rot(k_ref[0, :, cols]).astype(BF16)
        vo_ref[0, h] = v_ref[0, :, cols].astype(BF16)


def _rope(p_d, cos_t, sin_t, n_heads):
    bsz, t, _ = p_d.shape
    rb = ROW_BLOCK
    width = n_heads * LANES

    def in_spec(part):
        return pl.BlockSpec((1, rb, width), lambda b, i: (b, i, part))

    tab = pl.BlockSpec((rb, LANES), lambda b, i: (i, 0))
    out = pl.BlockSpec((1, n_heads, rb, LANES), lambda b, i: (b, 0, i, 0))
    shape = jax.ShapeDtypeStruct((bsz, n_heads, t, LANES), BF16)
    return pl.pallas_call(
        functools.partial(_rope_kernel, n_heads=n_heads),
        grid=(bsz, t // rb),
        in_specs=[in_spec(0), in_spec(1), in_spec(2), tab, tab],
        out_specs=[out, out, out],
        out_shape=[shape, shape, shape],
        compiler_params=_params("parallel", "parallel"),
        name="diff_rope",
    )(p_d, p_d, p_d, cos_t, sin_t)


def _attn_kernel(q_ref, k_ref, v_ref, lam_ref, nw_ref, o_ref, *, key_chunks):
    q = q_ref[0, 0]
    tq = q.shape[0]
    lane = lax.broadcasted_iota(jnp.int32, q.shape, 1)
    q_maps = (jnp.where(lane < DIFF_DQK, q, jnp.zeros_like(q)),
              jnp.where(lane >= DIFF_DQK, q, jnp.zeros_like(q)))
    lp = lam_ref[...]
    lam = (jnp.exp(jnp.sum(lp[0:1] * lp[1:2], axis=-1, keepdims=True))
           - jnp.exp(jnp.sum(lp[2:3] * lp[3:4], axis=-1, keepdims=True)) + LAM_INIT)
    m = [jnp.full((tq, 1), -jnp.inf, F32) for _ in range(2)]
    l = [jnp.zeros((tq, 1), F32) for _ in range(2)]
    acc = [jnp.zeros((tq, DIFF_DV), F32) for _ in range(2)]
    for start, size in key_chunks:
        k = k_ref[0, 0, start:start + size, :]
        v = v_ref[0, 0, start:start + size, :]
        for mi in range(2):
            s = lax.dot_general(q_maps[mi], k, _NT, preferred_element_type=F32)
            m_new = jnp.maximum(m[mi], jnp.max(s, axis=-1, keepdims=True))
            alpha = jnp.exp(m[mi] - m_new)
            e = jnp.exp(s - m_new)
            l[mi] = alpha * l[mi] + jnp.sum(e, axis=-1, keepdims=True)
            acc[mi] = alpha * acc[mi] + jnp.dot(e.astype(BF16), v, preferred_element_type=F32)
            m[mi] = m_new
    o = acc[0] / l[0] - lam * (acc[1] / l[1])
    ms = jnp.mean(o * o, axis=-1, keepdims=True)
    o_ref[0] = (o * lax.rsqrt(ms + 1e-5) * nw_ref[...] * (1.0 - LAM_INIT)).astype(BF16)


def _attention(qr, kr, vr, lam8, diff_norm_w, s):
    bsz, nh, t, _ = kr.shape
    tq = _tile(s, 512, ROW_BLOCK)
    kc = _tile(s, 1024, ROW_BLOCK)
    key_chunks = [(st, kc) for st in range(0, s, kc)] + [(s, t - s)]
    return pl.pallas_call(
        functools.partial(_attn_kernel, key_chunks=tuple(key_chunks)),
        grid=(bsz, nh, s // tq),
        in_specs=[pl.BlockSpec((1, 1, tq, LANES), lambda b, h, i: (b, h, i, 0)),
                  pl.BlockSpec((1, 1, t, LANES), lambda b, h, i: (b, h, 0, 0)),
                  pl.BlockSpec((1, 1, t, LANES), lambda b, h, i: (b, h, 0, 0)),
                  pl.BlockSpec((SUBLANES, LANES), lambda b, h, i: (0, 0)),
                  pl.BlockSpec((1, DIFF_DV), lambda b, h, i: (0, 0))],
        out_specs=pl.BlockSpec((1, tq, DIFF_DV), lambda b, h, i: (b, i, h)),
        out_shape=jax.ShapeDtypeStruct((bsz, s, nh * DIFF_DV), BF16),
        compiler_params=_params("parallel", "parallel", "arbitrary"),
        name="diff_attn",
    )(qr, kr, vr, lam8, diff_norm_w.reshape(1, DIFF_DV))


def _merge_kernel(ya_ref, yb_ref, wa_ref, wb_ref, ga_ref, gb_ref, o_ref):
    pa = jnp.dot(ya_ref[0], wa_ref[...], preferred_element_type=F32)
    pb = jnp.dot(yb_ref[0], wb_ref[...], preferred_element_type=F32)
    o_ref[0] = (jax.nn.sigmoid(ga_ref[0]) * pa + jax.nn.sigmoid(gb_ref[0]) * pb).astype(BF16)


def _merge(y_a, y_b, wa, wb, p3, ga_col, gb_col):
    bsz, s, ka = y_a.shape
    kb = y_b.shape[2]
    d = wa.shape[1]
    tm = _tile(s, 512, ROW_BLOCK)
    tn = _tile(math.gcd(d, ga_col, gb_col), 512, LANES)
    return pl.pallas_call(
        _merge_kernel,
        grid=(bsz, s // tm, d // tn),
        in_specs=[pl.BlockSpec((1, tm, ka), lambda b, i, j: (b, i, 0)),
                  pl.BlockSpec((1, tm, kb), lambda b, i, j: (b, i, 0)),
                  pl.BlockSpec((ka, tn), lambda b, i, j: (0, j)),
                  pl.BlockSpec((kb, tn), lambda b, i, j: (0, j)),
                  pl.BlockSpec((1, tm, tn), lambda b, i, j: (b, i, ga_col // tn + j)),
                  pl.BlockSpec((1, tm, tn), lambda b, i, j: (b, i, gb_col // tn + j))],
        out_specs=pl.BlockSpec((1, tm, tn), lambda b, i, j: (b, i, j)),
        out_shape=jax.ShapeDtypeStruct((bsz, s, d), BF16),
        compiler_params=_params("parallel", "parallel", "arbitrary"),
        name="merge",
    )(y_a, y_b, wa, wb, p3, p3)


def _layernorm_rows(ref, g, bvec, emit, rows_per_step=32):
    tm = ref.shape[0]

    def body(r, carry):
        rows = pl.ds(pl.multiple_of(r * rows_per_step, rows_per_step), rows_per_step)
        x = ref[rows, :]
        mu = jnp.mean(x, axis=-1, keepdims=True)
        xc = x - mu
        var = jnp.mean(xc * xc, axis=-1, keepdims=True)
        y = xc * lax.rsqrt(var + LN_EPS) * g + bvec
        ref[rows, :] = y
        emit(rows, y)
        return carry

    lax.fori_loop(0, tm // rows_per_step, body, 0)


def _proj_ln_kernel(a_ref, w_ref, x_ref, gt_ref, g_ref, b_ref, sc_ref, sh_ref,
                    o_ref, h_ref, *, alpha, n_j, tn, rows_per_batch):
    i = pl.program_id(0)
    j = pl.program_id(1)
    tm = a_ref.shape[0]
    b = (i * tm) // rows_per_batch
    mix = jnp.dot(a_ref[...], w_ref[...], preferred_element_type=F32)
    r = alpha * x_ref[...] + gt_ref[pl.ds(b, 1), :] * mix
    for jj in range(n_j):
        @pl.when(j == jj)
        def _(jj=jj):
            o_ref[:, jj * tn:(jj + 1) * tn] = r

    @pl.when(j == n_j - 1)
    def _():
        if h_ref is None:
            _layernorm_rows(o_ref, g_ref[...], b_ref[...], lambda rows, y: None)
        else:
            sc = 1.0 + sc_ref[pl.ds(b, 1), :]
            sh = sh_ref[pl.ds(b, 1), :]

            def emit(rows, y):
                h_ref[rows, :] = (y * sc + sh).astype(BF16)

            _layernorm_rows(o_ref, g_ref[...], b_ref[...], emit)


def _proj_ln_kernel_noh(a_ref, w_ref, x_ref, gt_ref, g_ref, b_ref, o_ref, **kw):
    _proj_ln_kernel(a_ref, w_ref, x_ref, gt_ref, g_ref, b_ref, None, None, o_ref, None, **kw)


def _proj_ln(a, w, x, mod, gt_chunk, ln_g, ln_b, rows_per_batch, *, alpha, tm, tn, name,
             next_mod_chunks=None):
    m, k = a.shape
    d = w.shape[1]
    n_j = d // tn
    kw = dict(alpha=alpha, n_j=n_j, tn=tn, rows_per_batch=rows_per_batch)
    in_specs = [pl.BlockSpec((tm, k), lambda i, j: (i, 0)),
                pl.BlockSpec((k, tn), lambda i, j: (0, j)),
                pl.BlockSpec((tm, tn), lambda i, j: (i, j)),
                pl.BlockSpec((SUBLANES, tn), lambda i, j: (0, gt_chunk * n_j + j)),
                pl.BlockSpec((1, d), lambda i, j: (0, 0)),
                pl.BlockSpec((1, d), lambda i, j: (0, 0))]
    args = [a, w, x, mod, ln_g.reshape(1, d), ln_b.reshape(1, d)]
    row_spec = pl.BlockSpec((tm, d), lambda i, j: (i, 0))
    if next_mod_chunks is None:
        kernel = functools.partial(_proj_ln_kernel_noh, **kw)
        out_specs = row_spec
        out_shape = jax.ShapeDtypeStruct((m, d), F32)
    else:
        sh_chunk, sc_chunk = next_mod_chunks
        kernel = functools.partial(_proj_ln_kernel, **kw)
        in_specs += [pl.BlockSpec((SUBLANES, d), lambda i, j: (0, sc_chunk)),
                     pl.BlockSpec((SUBLANES, d), lambda i, j: (0, sh_chunk))]
        args += [mod, mod]
        out_specs = [row_spec, row_spec]
        out_shape = [jax.ShapeDtypeStruct((m, d), F32), jax.ShapeDtypeStruct((m, d), BF16)]
    return pl.pallas_call(
        kernel,
        grid=(m // tm, n_j),
        in_specs=in_specs,
        out_specs=out_specs,
        out_shape=out_shape,
        compiler_params=_params("parallel", "arbitrary"),
        name=name,
    )(*args)


def _ffn_up_kernel(h_ref, wg_ref, wu_ref, o_ref):
    h = h_ref[...]
    g = jnp.dot(h, wg_ref[...], preferred_element_type=F32)
    u = jnp.dot(h, wu_ref[...], preferred_element_type=F32)
    o_ref[...] = (_silu(g) * u).astype(BF16)


def _ffn_up(h, wg, wu):
    m, d = h.shape
    f = wg.shape[1]
    tm = _tile(m, 1024, ROW_BLOCK)
    tn = _tile(f, 512, LANES)
    return pl.pallas_call(
        _ffn_up_kernel,
        grid=(m // tm, f // tn),
        in_specs=[pl.BlockSpec((tm, d), lambda i, j: (i, 0)),
                  pl.BlockSpec((d, tn), lambda i, j: (0, j)),
                  pl.BlockSpec((d, tn), lambda i, j: (0, j))],
        out_specs=pl.BlockSpec((tm, tn), lambda i, j: (i, j)),
        out_shape=jax.ShapeDtypeStruct((m, f), BF16),
        compiler_params=_params("parallel", "arbitrary"),
        name="ffn_up",
    )(h, wg, wu)


def _rope_tables(s, c):
    quarter = DIFF_DQK // 4
    inv_freq = ROPE_BASE ** (-jnp.arange(quarter, dtype=F32) / quarter)
    pos = jnp.arange(s, dtype=jnp.int32)
    lane = jnp.arange(LANES)
    use_col = (lane % DIFF_DQK) >= (DIFF_DQK // 2)
    p = jnp.where(use_col[None, :], (pos % GRID_W)[:, None], (pos // GRID_W)[:, None]).astype(F32)
    ang = p * inv_freq[lane % quarter][None, :]
    sign = jnp.where((lane % (2 * quarter)) < quarter, -1.0, 1.0)[None, :]
    cos = jnp.concatenate([jnp.cos(ang), jnp.ones((c, LANES), F32)], axis=0)
    sin = jnp.concatenate([jnp.sin(ang) * sign, jnp.zeros((c, LANES), F32)], axis=0)
    return cos, sin


def kernel(x, c, ctx, c_ctx, w_ada, b_ada, w_in, conv_w, a_log, dt_bias, gdn_norm_w, lam_q1, lam_k1,
           lam_q2, lam_k2, diff_norm_w, w_proj_a, w_proj_b, w_out, ln1_g, ln1_b, w_gate, w_up, w_down,
           ln2_g, ln2_b):
    bsz, s, d = x.shape
    cl = ctx.shape[1]
    t = s + cl
    nh = d // HEAD_W
    qk_w = nh * GDN_DK
    v_w = nh * GDN_DV
    dqk_w = nh * 2 * DIFF_DQK
    dv_w = nh * DIFF_DV
    alpha = (2.0 * w_ada.shape[0]) ** 0.25
    assert w_ada.shape[0] == 1 and s % ROW_BLOCK == 0 and cl % ROW_BLOCK == 0 and 8 * nh <= LANES

    c8 = jnp.zeros((SUBLANES, d), F32).at[:bsz].set(c).at[bsz].set(c_ctx)
    mod = _ada(c8, w_ada[0], b_ada[0])

    h_all = _modulate(x, ctx, mod, bsz)

    w_l = w_in[0]
    z_col = 2 * qk_w + v_w
    ab_start = z_col + v_w
    d_start = ab_start + 4 * nh
    n_d = 2 * dqk_w + dv_w + 2 * d
    h2d = h_all.reshape(bsz * t, d)
    tm_in = _tile(bsz * t, 1088, 2 * SUBLANES)
    p_g = _in_proj(h2d, w_l, 0, ab_start, tm=tm_in, tn=_tile(ab_start, 512, LANES),
                   name="in_proj_gdn").reshape(bsz, t, ab_start)
    p_ab = _in_proj(h2d, w_l, ab_start, LANES, tm=tm_in, tn=LANES,
                    name="in_proj_ab").reshape(bsz, t, LANES)
    p_d = _in_proj(h2d, w_l, d_start, n_d, tm=tm_in,
                   tn=_tile(math.gcd(n_d, d_start - d_start % LANES), 512, LANES),
                   name="in_proj_diff").reshape(bsz, t, n_d)
    ga_col = 2 * dqk_w + dv_w
    gb_col = ga_col + d

    w2 = 2 * nh
    alog128 = jnp.pad(a_log[0].reshape(1, w2), ((0, 0), (0, LANES - w2)))
    dtb128 = jnp.pad(dt_bias[0].reshape(1, w2), ((0, 0), (0, LANES - w2)))
    gb = _gates(p_ab, alog128, dtb128, w2)
    gr = jnp.swapaxes(gb[:, :, :w2], 1, 2)
    g_tot = gb[:, ::GDN_CHUNK, 3 * w2:4 * w2].reshape(-1)
    qkv = _gdn_conv(p_g, conv_w[0], nh, s)
    u, w_, qd, qk, kdt = _gdn_intra(qkv, gb, gr, nh)
    o_f, o_b = _gdn_scan(g_tot, u, w_, qd, qk, kdt, nh, s)
    y_a = _gdn_out(o_f, o_b, p_g, z_col // v_w, gdn_norm_w[0], nh, s)

    cos_t, sin_t = _rope_tables(s, cl)
    qr, kr, vr = _rope(p_d, cos_t, sin_t, nh)
    lam8 = jnp.zeros((SUBLANES, LANES), F32)
    for r, vec in enumerate((lam_q1, lam_k1, lam_q2, lam_k2)):
        lam8 = lam8.at[r, :DIFF_DQK].set(vec[0])
    y_b = _attention(qr, kr, vr, lam8, diff_norm_w[0], s)

    ymg = _merge(y_a, y_b, w_proj_a[0].astype(BF16), w_proj_b[0].astype(BF16), p_d, ga_col, gb_col)
    m = bsz * s
    x1, h2 = _proj_ln(ymg.reshape(m, d), w_out[0].astype(BF16), x.reshape(m, d), mod, 2,
                      ln1_g[0], ln1_b[0], s, alpha=alpha, tm=_tile(s, 512, ROW_BLOCK),
                      tn=_tile(d, 512, LANES), name="out_proj_ln", next_mod_chunks=(3, 4))

    hff = _ffn_up(h2, w_gate[0].astype(BF16), w_up[0].astype(BF16))
    out = _proj_ln(hff, w_down[0].astype(BF16), x1, mod, 5, ln2_g[0], ln2_b[0], s, alpha=alpha,
                   tm=_tile(s, 512, ROW_BLOCK), tn=_tile(d, 256, LANES), name="ffn_down_ln")
    return out.reshape(bsz, s, d)
```

```python
import functools
import math

import jax
import jax.numpy as jnp
from jax import lax
from jax.experimental import pallas as pl
from jax.experimental.pallas import tpu as pltpu

F32 = jnp.float32
BF16 = jnp.bfloat16

GRID_W = 64
HEAD_W = 256
GDN_DK = 128
GDN_DV = 128
GDN_CHUNK = 64
DIFF_DQK = 64
DIFF_DV = 128
ROPE_BASE = 10000.0
LN_EPS = 1e-6
LAM_INIT = 0.8 - 0.6 * math.exp(-0.3 * 0)

LANES = 128
SUBLANES = 8
PAIR = 2 * GDN_CHUNK
ROW_BLOCK = 256
VMEM_LIMIT = 56 * 1024 * 1024

_NT = (((1,), (1,)), ((), ()))


def _params(*sem):
    return pltpu.CompilerParams(dimension_semantics=sem, vmem_limit_bytes=VMEM_LIMIT)


def _tile(n, target, quantum):
    best = None
    t = quantum
    while t <= min(n, target):
        if n % t == 0:
            best = t
        t += quantum
    assert best is not None, (n, target, quantum)
    return best


def _silu(x):
    return x * jax.nn.sigmoid(x)


def _ada_kernel(c_ref, w_ref, b_ref, o_ref):
    c = _silu(c_ref[...])
    o_ref[...] = jnp.dot(c.astype(BF16), w_ref[...].astype(BF16),
                         preferred_element_type=F32) + b_ref[...]


def _ada(c8, w_ada, b_ada):
    d, n = w_ada.shape
    tn = _tile(n, 512, LANES)
    return pl.pallas_call(
        _ada_kernel,
        grid=(n // tn,),
        in_specs=[pl.BlockSpec((SUBLANES, d), lambda j: (0, 0)),
                  pl.BlockSpec((d, tn), lambda j: (0, j)),
                  pl.BlockSpec((1, tn), lambda j: (0, j))],
        out_specs=pl.BlockSpec((SUBLANES, tn), lambda j: (0, j)),
        out_shape=jax.ShapeDtypeStruct((SUBLANES, n), F32),
        compiler_params=_params("parallel"),
        name="ada",
    )(c8, w_ada, b_ada.reshape(1, n))


def _modulate_kernel(x_ref, ctx_ref, sh_ref, sc_ref, o_ref, *, n_lat, ctx_row):
    b = pl.program_id(0)
    i = pl.program_id(1)

    @pl.when(i < n_lat)
    def _():
        sc = sc_ref[pl.ds(b, 1), :]
        sh = sh_ref[pl.ds(b, 1), :]
        o_ref[0] = (x_ref[0] * (1.0 + sc) + sh).astype(BF16)

    @pl.when(i >= n_lat)
    def _():
        sc = sc_ref[ctx_row:ctx_row + 1, :]
        sh = sh_ref[ctx_row:ctx_row + 1, :]
        o_ref[0] = (ctx_ref[0] * (1.0 + sc) + sh).astype(BF16)


def _modulate(x, ctx, mod, ctx_row):
    bsz, s, d = x.shape
    c = ctx.shape[1]
    t = s + c
    rb = ROW_BLOCK
    n_lat = s // rb
    return pl.pallas_call(
        functools.partial(_modulate_kernel, n_lat=n_lat, ctx_row=ctx_row),
        grid=(bsz, t // rb),
        in_specs=[pl.BlockSpec((1, rb, d), lambda b, i: (b, jnp.minimum(i, n_lat - 1), 0)),
                  pl.BlockSpec((1, rb, d), lambda b, i: (b, jnp.maximum(i - n_lat, 0), 0)),
                  pl.BlockSpec((SUBLANES, d), lambda b, i: (0, 0)),
                  pl.BlockSpec((SUBLANES, d), lambda b, i: (0, 1))],
        out_specs=pl.BlockSpec((1, rb, d), lambda b, i: (b, i, 0)),
        out_shape=jax.ShapeDtypeStruct((bsz, t, d), BF16),
        compiler_params=_params("parallel", "arbitrary"),
        name="modulate",
    )(x, ctx, mod, mod)


def _in_proj_kernel(*refs, shift, chunk):
    if shift:
        a_ref, w_ref, wn_ref, o_ref, wb_ref = refs
    else:
        a_ref, w_ref, o_ref, wb_ref = refs
    tn = w_ref.shape[0]

    @pl.when(pl.program_id(1) == 0)
    def _():
        def body(r, carry):
            dst = pl.multiple_of(r * chunk, chunk)
            wb_ref[pl.ds(dst, chunk), :] = w_ref[pl.ds(dst + shift, chunk), :].astype(BF16)
            return carry

        lax.fori_loop(0, (tn - shift) // chunk, body, 0)
        for r in range(shift // chunk):
            wb_ref[tn - shift + r * chunk:tn - shift + (r + 1) * chunk, :] = (
                wn_ref[r * chunk:(r + 1) * chunk, :].astype(BF16))

    o_ref[...] = lax.dot_general(a_ref[...], wb_ref[...], _NT, preferred_element_type=F32)


def _in_proj(a, w_t, row0, n, *, tm, tn, name):
    m, k = a.shape
    shift = row0 % LANES
    base = row0 - shift
    assert base % tn == 0 and n % tn == 0
    chunk = math.gcd(tn, shift) if shift else math.gcd(tn, 64)
    in_specs = [pl.BlockSpec((tm, k), lambda j, i: (i, 0)),
                pl.BlockSpec((tn, k), lambda j, i: (base // tn + j, 0))]
    args = [a, w_t]
    if shift:
        in_specs.append(pl.BlockSpec((LANES, k), lambda j, i: ((base + (j + 1) * tn) // LANES, 0)))
        args.append(w_t)
    return pl.pallas_call(
        functools.partial(_in_proj_kernel, shift=shift, chunk=chunk),
        grid=(n // tn, m // tm),
        in_specs=in_specs,
        out_specs=pl.BlockSpec((tm, tn), lambda j, i: (i, j)),
        out_shape=jax.ShapeDtypeStruct((m, n), F32),
        scratch_shapes=[pltpu.VMEM((tn, k), BF16)],
        compiler_params=_params("parallel", "arbitrary"),
        name=name,
    )(*args)


def _gates_kernel(p_ref, alog_ref, dtb_ref, o_ref, *, w2):
    p = p_ref[0]
    rb = p.shape[0]
    lane = lax.broadcasted_iota(jnp.int32, p.shape, 1)
    row = lax.broadcasted_iota(jnp.int32, p.shape, 0) % GDN_CHUNK
    xa = p + dtb_ref[...]
    softplus = jnp.maximum(xa, 0.0) + jnp.log(1.0 + jnp.exp(-jnp.abs(xa)))
    g = jnp.where(lane < w2, -jnp.exp(alog_ref[...]) * softplus, 0.0)
    pre = g
    suf = g
    step = 1
    while step < GDN_CHUNK:
        pre = pre + jnp.where(row >= step, pltpu.roll(pre, step, axis=0), 0.0)
        suf = suf + jnp.where(row < GDN_CHUNK - step, pltpu.roll(suf, rb - step, axis=0), 0.0)
        step *= 2
    tot = pre + suf - g
    cum = jnp.where(lane < w2 // 2, pre, suf)
    beta = jax.nn.sigmoid(p)
    out = jnp.where(lane < w2, cum,
                    jnp.where(lane < 2 * w2, beta,
                              jnp.where(lane < 3 * w2, pltpu.roll(tot, 2 * w2, axis=1),
                                        pltpu.roll(jnp.exp(tot), 3 * w2, axis=1))))
    o_ref[0] = jnp.where(lane < 4 * w2, out, 0.0)


def _gates(p_ab, alog128, dtb128, w2):
    bsz, t, _ = p_ab.shape
    rb = ROW_BLOCK
    return pl.pallas_call(
        functools.partial(_gates_kernel, w2=w2),
        grid=(bsz, t // rb),
        in_specs=[pl.BlockSpec((1, rb, LANES), lambda b, i: (b, i, 0)),
                  pl.BlockSpec((1, LANES), lambda b, i: (0, 0)),
                  pl.BlockSpec((1, LANES), lambda b, i: (0, 0))],
        out_specs=pl.BlockSpec((1, rb, LANES), lambda b, i: (b, i, 0)),
        out_shape=jax.ShapeDtypeStruct((bsz, t, LANES), F32),
        compiler_params=_params("parallel", "parallel"),
        name="gdn_gates",
    )(p_ab, alog128, dtb128)


def _conv_kernel(x_ref, cw_ref, o_ref, *, n_heads, cb, n_lat, n_blk, taps):
    j = pl.program_id(1)
    t = x_ref.shape[1]
    width = x_ref.shape[2]
    rb = ROW_BLOCK
    halo = SUBLANES
    pad = taps // 2
    win_rows = rb + 2 * halo
    w = cw_ref[...]
    is_qk = j * cb < 2 * n_heads
    q_scale = jnp.where(j * cb < n_heads, GDN_DK ** -0.5, 1.0)

    def body(r, carry):
        base = pl.multiple_of(r * rb, rb)
        top_ok = jnp.logical_and(r != 0, r != n_lat)
        bot_ok = jnp.logical_and(r != n_lat - 1, r != n_blk - 1)
        top = x_ref[0, pl.ds(pl.multiple_of(jnp.maximum(base - halo, 0), halo), halo), :]
        bot = x_ref[0, pl.ds(pl.multiple_of(jnp.minimum(base + rb, t - halo), halo), halo), :]
        win = jnp.concatenate([jnp.where(top_ok, top, 0.0), x_ref[0, pl.ds(base, rb), :],
                               jnp.where(bot_ok, bot, 0.0)], axis=0)
        acc = jnp.zeros((win_rows, width), F32)
        for tap in range(taps):
            shift = (pad - tap) % win_rows
            shifted = win if shift == 0 else pltpu.roll(win, shift, axis=0)
            acc = acc + shifted * w[tap:tap + 1, :]
        y = _silu(acc[halo:halo + rb, :])
        for c in range(cb):
            yc = y[:, c * LANES:(c + 1) * LANES]
            ss = jnp.sum(yc * yc, axis=-1, keepdims=True)
            scale = jnp.where(is_qk, lax.rsqrt(ss + 1e-6) * q_scale, 1.0)
            o_ref[0, c, pl.ds(base, rb), :] = yc * scale
        return carry

    lax.fori_loop(0, n_blk, body, 0)


def _gdn_conv(p_g, conv_w, n_heads, s):
    bsz, t, _ = p_g.shape
    taps = conv_w.shape[0]
    n_cols = 3 * n_heads
    cb = _tile(n_heads, 2, 1)
    return pl.pallas_call(
        functools.partial(_conv_kernel, n_heads=n_heads, cb=cb, n_lat=s // ROW_BLOCK,
                          n_blk=t // ROW_BLOCK, taps=taps),
        grid=(bsz, n_cols // cb),
        in_specs=[pl.BlockSpec((1, t, cb * LANES), lambda b, j: (b, 0, j)),
                  pl.BlockSpec((taps, cb * LANES), lambda b, j: (0, j))],
        out_specs=pl.BlockSpec((1, cb, t, LANES), lambda b, j: (b, j, 0, 0)),
        out_shape=jax.ShapeDtypeStruct((bsz, n_cols, t, LANES), F32),
        compiler_params=_params("parallel", "parallel"),
        name="gdn_conv",
    )(p_g, conv_w)


def _gdn_intra_kernel(q_ref, k_ref, v_ref, gb_ref, gr_ref,
                      u_ref, w_ref, qd_ref, qk_ref, kdt_ref, *, n_heads, hb):
    hblk = pl.program_id(1)
    w2 = 2 * n_heads
    rb = q_ref.shape[2]
    gbv = gb_ref[0]
    lane = lax.broadcasted_iota(jnp.int32, gbv.shape, 1)

    def column(idx):
        return jnp.sum(jnp.where(lane == idx, gbv, 0.0), axis=-1, keepdims=True)

    ri = lax.broadcasted_iota(jnp.int32, (PAIR, PAIR), 0)
    ci = lax.broadcasted_iota(jnp.int32, (PAIR, PAIR), 1)
    same = (ri >= GDN_CHUNK) == (ci >= GDN_CHUNK)
    eye_f32 = jnp.where(ri == ci, 1.0, 0.0)
    eye = eye_f32.astype(BF16)
    levels = int(math.log2(GDN_CHUNK))
    off_blocks = ([], [])
    for lvl in range(levels):
        half = 1 << lvl
        same_blk = (ri >> (lvl + 1)) == (ci >> (lvl + 1))
        r_hi = (ri & half) != 0
        c_hi = (ci & half) != 0
        off_blocks[0].append(same_blk & r_hi & jnp.logical_not(c_hi))
        off_blocks[1].append(same_blk & c_hi & jnp.logical_not(r_hi))

    incls = (jnp.logical_and(same, ri >= ci), jnp.logical_and(same, ri <= ci))
    stricts = (jnp.logical_and(same, ri > ci), jnp.logical_and(same, ri < ci))

    probs = []
    for hh in range(hb):
        head = hblk * hb + hh
        gcols = [column(d * n_heads + head) for d in range(2)]
        bcols = [column(w2 + d * n_heads + head) for d in range(2)]
        tcols = [column(2 * w2 + d * n_heads + head) for d in range(2)]
        grows = [gr_ref[0, pl.ds(d * n_heads + head, 1), :] for d in range(2)]
        for p in range(rb // PAIR):
            rows = slice(p * PAIR, (p + 1) * PAIR)
            q = q_ref[0, hh, rows, :]
            k = k_ref[0, hh, rows, :]
            kbf = k.astype(BF16)
            kk = lax.dot_general(kbf, kbf, _NT, preferred_element_type=F32)
            qkm = lax.dot_general(q.astype(BF16), kbf, _NT, preferred_element_type=F32)
            for d in range(2):
                gc = gcols[d][rows]
                bc = bcols[d][rows]
                dec = jnp.where(incls[d], jnp.exp(jnp.where(incls[d], gc - grows[d][:, rows], 0.0)), 0.0)
                a = jnp.where(stricts[d], kk * bc * dec, 0.0)
                qk_ref[0, d, hh, rows, :] = jnp.where(incls[d], qkm * dec, 0.0).astype(BF16)
                probs.append(dict(hh=hh, rows=rows, d=d, a=a, gc=gc, bc=bc, tc=tcols[d][rows]))

    minv = [eye_f32 - jnp.where(off_blocks[pr["d"]][0], pr["a"], 0.0) for pr in probs]
    for lvl in range(1, levels):
        mb = [m.astype(BF16) for m in minv]
        t1 = [jnp.dot(jnp.where(off_blocks[pr["d"]][lvl], pr["a"], 0.0).astype(BF16), mb[i],
                      preferred_element_type=F32).astype(BF16) for i, pr in enumerate(probs)]
        minv = [minv[i] - jnp.dot(mb[i], t1[i], preferred_element_type=F32) for i in range(len(probs))]

    for i, pr in enumerate(probs):
        hh, rows, d, gc, bc = pr["hh"], pr["rows"], pr["d"], pr["gc"], pr["bc"]
        q = q_ref[0, hh, rows, :]
        k = k_ref[0, hh, rows, :]
        v = v_ref[0, hh, rows, :]
        eg = jnp.exp(gc)
        vb = v * bc
        kbg = k * (bc * eg)
        rhs = jnp.concatenate([vb, kbg], axis=1).astype(BF16)
        uw = jnp.dot((minv[i] - eye_f32).astype(BF16), rhs, preferred_element_type=F32)
        u_ref[0, d, hh, rows, :] = vb + uw[:, :GDN_DV]
        w_ref[0, d, hh, rows, :] = (kbg + uw[:, GDN_DV:]).astype(BF16)
        qd_ref[0, d, hh, rows, :] = (q * eg).astype(BF16)
        kd = (k * jnp.exp(pr["tc"] - gc)).astype(BF16)
        kdt_ref[0, d, hh, rows, :] = lax.dot_general(
            eye, kd, _NT, preferred_element_type=F32).astype(BF16)


def _gdn_intra(qkv, gb, gr, n_heads):
    bsz, _, t, _ = qkv.shape
    rb = ROW_BLOCK
    nh = n_heads
    hb = _tile(nh, 4, 1)
    nhb = nh // hb

    def qkv_spec(part):
        return pl.BlockSpec((1, hb, rb, LANES), lambda b, h, i: (b, part * nhb + h, i, 0))

    out_spec = pl.BlockSpec((1, 2, hb, rb, LANES), lambda b, h, i: (b, 0, h, i, 0))
    shape = (bsz, 2, nh, t, LANES)
    return pl.pallas_call(
        functools.partial(_gdn_intra_kernel, n_heads=nh, hb=hb),
        grid=(bsz, nhb, t // rb),
        in_specs=[qkv_spec(0), qkv_spec(1), qkv_spec(2),
                  pl.BlockSpec((1, rb, LANES), lambda b, h, i: (b, i, 0)),
                  pl.BlockSpec((1, gr.shape[1], rb), lambda b, h, i: (b, 0, i))],
        out_specs=[out_spec] * 5,
        out_shape=[jax.ShapeDtypeStruct(shape, F32)] + [jax.ShapeDtypeStruct(shape, BF16)] * 4,
        compiler_params=_params("parallel", "parallel", "parallel"),
        name="gdn_intra",
    )(qkv, qkv, qkv, gb, gr)


def _gdn_scan_kernel(gt_ref, uf, wf, qdf, qkf, kdtf, ub, wb, qdb, qkb, kdtb,
                     of_ref, ob_ref, s_ref, *, n_heads, hb, n_lat_blk, n_blk):
    b = pl.program_id(0)
    hblk = pl.program_id(1)
    i = pl.program_id(2)
    rb = uf.shape[3]
    n_ch = rb // GDN_CHUNK
    w2 = 2 * n_heads

    @pl.when(i == 0)
    def _():
        s_ref[...] = jnp.zeros(s_ref.shape, F32)

    blk_f = (i + n_lat_blk) % n_blk
    blk_b = n_blk - 1 - i
    zeros = jnp.zeros((GDN_CHUNK, GDN_DV), BF16)
    dirs = ((0, uf, wf, qdf, qkf, kdtf, of_ref, blk_f),
            (1, ub, wb, qdb, qkb, kdtb, ob_ref, blk_b))
    for c in range(n_ch):
        chains = []
        for d, u_r, w_r, qd_r, qk_r, kdt_r, o_r, blk in dirs:
            cc = c if d == 0 else n_ch - 1 - c
            rows = slice(cc * GDN_CHUNK, (cc + 1) * GDN_CHUNK)
            pair_rows = slice((cc // 2) * PAIR, (cc // 2 + 1) * PAIR)
            for hh in range(hb):
                g_tot = gt_ref[(b * (n_blk * n_ch) + blk * n_ch + cc) * w2 + d * n_heads + hblk * hb + hh]
                chains.append((d, hh, cc, rows, pair_rows, u_r, w_r, qd_r, qk_r, kdt_r, o_r, g_tot))
        states = [s_ref[d, hh] for d, hh, *_ in chains]
        m1s = [jnp.dot(jnp.concatenate([w_r[0, 0, hh, rows, :], qd_r[0, 0, hh, rows, :]], axis=0),
                       states[n].astype(BF16), preferred_element_type=F32)
               for n, (d, hh, cc, rows, pair_rows, u_r, w_r, qd_r, qk_r, kdt_r, o_r, g_tot) in enumerate(chains)]
        m2s = []
        for n, (d, hh, cc, rows, pair_rows, u_r, w_r, qd_r, qk_r, kdt_r, o_r, g_tot) in enumerate(chains):
            v_new = (u_r[0, 0, hh, rows, :] - m1s[n][:GDN_CHUNK]).astype(BF16)
            v_ext = (jnp.concatenate([v_new, zeros], axis=0) if cc % 2 == 0
                     else jnp.concatenate([zeros, v_new], axis=0))
            lhs2 = jnp.concatenate([qk_r[0, 0, hh, rows, :], kdt_r[0, 0, hh, pair_rows, :]], axis=0)
            m2s.append(jnp.dot(lhs2, v_ext, preferred_element_type=F32))
        for n, (d, hh, cc, rows, pair_rows, u_r, w_r, qd_r, qk_r, kdt_r, o_r, g_tot) in enumerate(chains):
            s_ref[d, hh] = states[n] * g_tot + m2s[n][GDN_CHUNK:]
            o_r[0, rows, hh * GDN_DV:(hh + 1) * GDN_DV] = m1s[n][GDN_CHUNK:] + m2s[n][:GDN_CHUNK]


def _gdn_scan(g_tot, u, w, qd, qk, kdt, n_heads, s):
    bsz, _, nh, t, _ = u.shape
    rb = ROW_BLOCK
    hb = _tile(nh, 8, 1)
    n_blk = t // rb
    n_lat_blk = s // rb

    def spec(d):
        if d == 0:
            return pl.BlockSpec((1, 1, hb, rb, LANES),
                                lambda b, h, i: (b, 0, h, (i + n_lat_blk) % n_blk, 0))
        return pl.BlockSpec((1, 1, hb, rb, LANES), lambda b, h, i: (b, 1, h, n_blk - 1 - i, 0))

    o_shape = jax.ShapeDtypeStruct((bsz, t, nh * GDN_DV), F32)
    return pl.pallas_call(
        functools.partial(_gdn_scan_kernel, n_heads=nh, hb=hb, n_lat_blk=n_lat_blk, n_blk=n_blk),
        grid=(bsz, nh // hb, n_blk),
        in_specs=[pl.BlockSpec(memory_space=pltpu.SMEM)] + [spec(0)] * 5 + [spec(1)] * 5,
        out_specs=[pl.BlockSpec((1, rb, hb * GDN_DV), lambda b, h, i: (b, (i + n_lat_blk) % n_blk, h)),
                   pl.BlockSpec((1, rb, hb * GDN_DV), lambda b, h, i: (b, n_blk - 1 - i, h))],
        out_shape=[o_shape, o_shape],
        scratch_shapes=[pltpu.VMEM((2, hb, GDN_DK, GDN_DV), F32)],
        compiler_params=_params("parallel", "parallel", "arbitrary"),
        name="gdn_scan",
    )(g_tot, u, w, qd, qk, kdt, u, w, qd, qk, kdt)


def _gdn_out_kernel(of_ref, ob_ref, z_ref, nw_ref, o_ref, *, n_heads):
    nw = nw_ref[...]
    for h in range(n_heads):
        cols = slice(h * GDN_DV, (h + 1) * GDN_DV)
        o = of_ref[0, :, cols] + ob_ref[0, :, cols]
        ms = jnp.mean(o * o, axis=-1, keepdims=True)
        o_ref[0, :, cols] = (o * lax.rsqrt(ms + 1e-6) * nw * _silu(z_ref[0, :, cols])).astype(BF16)


def _gdn_out(o_f, o_b, p3, z_col_blk, gdn_norm_w, n_heads, s):
    bsz = o_f.shape[0]
    width = n_heads * GDN_DV
    tm = ROW_BLOCK
    return pl.pallas_call(
        functools.partial(_gdn_out_kernel, n_heads=n_heads),
        grid=(bsz, s // tm),
        in_specs=[pl.BlockSpec((1, tm, width), lambda b, i: (b, i, 0)),
                  pl.BlockSpec((1, tm, width), lambda b, i: (b, i, 0)),
                  pl.BlockSpec((1, tm, width), lambda b, i: (b, i, z_col_blk)),
                  pl.BlockSpec((1, GDN_DV), lambda b, i: (0, 0))],
        out_specs=pl.BlockSpec((1, tm, width), lambda b, i: (b, i, 0)),
        out_shape=jax.ShapeDtypeStruct((bsz, s, width), BF16),
        compiler_params=_params("parallel", "parallel"),
        name="gdn_out",
    )(o_f, o_b, p3, gdn_norm_w.reshape(1, GDN_DV))


def _rope_kernel(q_ref, k_ref, v_ref, cos_ref, sin_ref, qo_ref, ko_ref, vo_ref, *, n_heads):
    cos = cos_ref[...]
    sin = sin_ref[...]
    lane = lax.broadcasted_iota(jnp.int32, cos.shape, 1)
    first_half = (lane % (DIFF_DQK // 2)) < (DIFF_DQK // 4)
    quarter = DIFF_DQK // 4

    def rot(x):
        partner = jnp.where(first_half, pltpu.roll(x, LANES - quarter, axis=1),
                            pltpu.roll(x, quarter, axis=1))
        return x * cos + partner * sin

    for h in range(n_heads):
        cols = slice(h * LANES, (h + 1) * LANES)
        qo_ref[0, h] = (rot(q_ref[0, :, cols]) * (DIFF_DQK ** -0.5)).astype(BF16)
        ko_ref[0, h] = rot(k_ref[0, :, cols]).astype(BF16)
        vo_ref[0, h] = v_ref[0, :, cols].astype(BF16)


def _rope(p_d, cos_t, sin_t, n_heads):
    bsz, t, _ = p_d.shape
    rb = ROW_BLOCK
    width = n_heads * LANES

    def in_spec(part):
        return pl.BlockSpec((1, rb, width), lambda b, i: (b, i, part))

    tab = pl.BlockSpec((rb, LANES), lambda b, i: (i, 0))
    out = pl.BlockSpec((1, n_heads, rb, LANES), lambda b, i: (b, 0, i, 0))
    shape = jax.ShapeDtypeStruct((bsz, n_heads, t, LANES), BF16)
    return pl.pallas_call(
        functools.partial(_rope_kernel, n_heads=n_heads),
        grid=(bsz, t // rb),
        in_specs=[in_spec(0), in_spec(1), in_spec(2), tab, tab],
        out_specs=[out, out, out],
        out_shape=[shape, shape, shape],
        compiler_params=_params("parallel", "parallel"),
        name="diff_rope",
    )(p_d, p_d, p_d, cos_t, sin_t)


def _attn_kernel(q_ref, k_ref, v_ref, lam_ref, nw_ref, o_ref, *, key_chunks):
    q = q_ref[0, 0]
    tq = q.shape[0]
    lane = lax.broadcasted_iota(jnp.int32, q.shape, 1)
    q_maps = (jnp.where(lane < DIFF_DQK, q, jnp.zeros_like(q)),
              jnp.where(lane >= DIFF_DQK, q, jnp.zeros_like(q)))
    lp = lam_ref[...]
    lam = (jnp.exp(jnp.sum(lp[0:1] * lp[1:2], axis=-1, keepdims=True))
           - jnp.exp(jnp.sum(lp[2:3] * lp[3:4], axis=-1, keepdims=True)) + LAM_INIT)
    m = [jnp.full((tq, 1), -jnp.inf, F32) for _ in range(2)]
    l = [jnp.zeros((tq, 1), F32) for _ in range(2)]
    acc = [jnp.zeros((tq, DIFF_DV), F32) for _ in range(2)]
    for start, size in key_chunks:
        k = k_ref[0, 0, start:start + size, :]
        v = v_ref[0, 0, start:start + size, :]
        for mi in range(2):
            s = lax.dot_general(q_maps[mi], k, _NT, preferred_element_type=F32)
            m_new = jnp.maximum(m[mi], jnp.max(s, axis=-1, keepdims=True))
            alpha = jnp.exp(m[mi] - m_new)
            e = jnp.exp(s - m_new)
            l[mi] = alpha * l[mi] + jnp.sum(e, axis=-1, keepdims=True)
            acc[mi] = alpha * acc[mi] + jnp.dot(e.astype(BF16), v, preferred_element_type=F32)
            m[mi] = m_new
    o = acc[0] / l[0] - lam * (acc[1] / l[1])
    ms = jnp.mean(o * o, axis=-1, keepdims=True)
    o_ref[0] = (o * lax.rsqrt(ms + 1e-5) * nw_ref[...] * (1.0 - LAM_INIT)).astype(BF16)


def _attention(qr, kr, vr, lam8, diff_norm_w, s):
    bsz, nh, t, _ = kr.shape
    tq = _tile(s, 512, ROW_BLOCK)
    kc = _tile(s, 1024, ROW_BLOCK)
    key_chunks = [(st, kc) for st in range(0, s, kc)] + [(s, t - s)]
    return pl.pallas_call(
        functools.partial(_attn_kernel, key_chunks=tuple(key_chunks)),
        grid=(bsz, nh, s // tq),
        in_specs=[pl.BlockSpec((1, 1, tq, LANES), lambda b, h, i: (b, h, i, 0)),
                  pl.BlockSpec((1, 1, t, LANES), lambda b, h, i: (b, h, 0, 0)),
                  pl.BlockSpec((1, 1, t, LANES), lambda b, h, i: (b, h, 0, 0)),
                  pl.BlockSpec((SUBLANES, LANES), lambda b, h, i: (0, 0)),
                  pl.BlockSpec((1, DIFF_DV), lambda b, h, i: (0, 0))],
        out_specs=pl.BlockSpec((1, tq, DIFF_DV), lambda b, h, i: (b, i, h)),
        out_shape=jax.ShapeDtypeStruct((bsz, s, nh * DIFF_DV), BF16),
        compiler_params=_params("parallel", "parallel", "arbitrary"),
        name="diff_attn",
    )(qr, kr, vr, lam8, diff_norm_w.reshape(1, DIFF_DV))


def _merge_kernel(ya_ref, yb_ref, wa_ref, wb_ref, ga_ref, gb_ref, o_ref):
    pa = jnp.dot(ya_ref[0], wa_ref[...], preferred_element_type=F32)
    pb = jnp.dot(yb_ref[0], wb_ref[...], preferred_element_type=F32)
    o_ref[0] = (jax.nn.sigmoid(ga_ref[0]) * pa + jax.nn.sigmoid(gb_ref[0]) * pb).astype(BF16)


def _merge(y_a, y_b, wa, wb, p3, ga_col, gb_col):
    bsz, s, ka = y_a.shape
    kb = y_b.shape[2]
    d = wa.shape[1]
    tm = _tile(s, 512, ROW_BLOCK)
    tn = _tile(math.gcd(d, ga_col, gb_col), 512, LANES)
    return pl.pallas_call(
        _merge_kernel,
        grid=(bsz, s // tm, d // tn),
        in_specs=[pl.BlockSpec((1, tm, ka), lambda b, i, j: (b, i, 0)),
                  pl.BlockSpec((1, tm, kb), lambda b, i, j: (b, i, 0)),
                  pl.BlockSpec((ka, tn), lambda b, i, j: (0, j)),
                  pl.BlockSpec((kb, tn), lambda b, i, j: (0, j)),
                  pl.BlockSpec((1, tm, tn), lambda b, i, j: (b, i, ga_col // tn + j)),
                  pl.BlockSpec((1, tm, tn), lambda b, i, j: (b, i, gb_col // tn + j))],
        out_specs=pl.BlockSpec((1, tm, tn), lambda b, i, j: (b, i, j)),
        out_shape=jax.ShapeDtypeStruct((bsz, s, d), BF16),
        compiler_params=_params("parallel", "parallel", "arbitrary"),
        name="merge",
    )(y_a, y_b, wa, wb, p3, p3)


def _layernorm_rows(ref, g, bvec, emit, rows_per_step=32):
    tm = ref.shape[0]

    def body(r, carry):
        rows = pl.ds(pl.multiple_of(r * rows_per_step, rows_per_step), rows_per_step)
        x = ref[rows, :]
        mu = jnp.mean(x, axis=-1, keepdims=True)
        xc = x - mu
        var = jnp.mean(xc * xc, axis=-1, keepdims=True)
        y = xc * lax.rsqrt(var + LN_EPS) * g + bvec
        ref[rows, :] = y
        emit(rows, y)
        return carry

    lax.fori_loop(0, tm // rows_per_step, body, 0)


def _proj_ln_kernel(a_ref, w_ref, x_ref, gt_ref, g_ref, b_ref, sc_ref, sh_ref,
                    o_ref, h_ref, *, alpha, n_j, tn, rows_per_batch):
    i = pl.program_id(0)
    j = pl.program_id(1)
    tm = a_ref.shape[0]
    b = (i * tm) // rows_per_batch
    mix = jnp.dot(a_ref[...], w_ref[...], preferred_element_type=F32)
    r = alpha * x_ref[...] + gt_ref[pl.ds(b, 1), :] * mix
    for jj in range(n_j):
        @pl.when(j == jj)
        def _(jj=jj):
            o_ref[:, jj * tn:(jj + 1) * tn] = r

    @pl.when(j == n_j - 1)
    def _():
        if h_ref is None:
            _layernorm_rows(o_ref, g_ref[...], b_ref[...], lambda rows, y: None)
        else:
            sc = 1.0 + sc_ref[pl.ds(b, 1), :]
            sh = sh_ref[pl.ds(b, 1), :]

            def emit(rows, y):
                h_ref[rows, :] = (y * sc + sh).astype(BF16)

            _layernorm_rows(o_ref, g_ref[...], b_ref[...], emit)


def _proj_ln_kernel_noh(a_ref, w_ref, x_ref, gt_ref, g_ref, b_ref, o_ref, **kw):
    _proj_ln_kernel(a_ref, w_ref, x_ref, gt_ref, g_ref, b_ref, None, None, o_ref, None, **kw)


def _proj_ln(a, w, x, mod, gt_chunk, ln_g, ln_b, rows_per_batch, *, alpha, tm, tn, name,
             next_mod_chunks=None):
    m, k = a.shape
    d = w.shape[1]
    n_j = d // tn
    kw = dict(alpha=alpha, n_j=n_j, tn=tn, rows_per_batch=rows_per_batch)
    in_specs = [pl.BlockSpec((tm, k), lambda i, j: (i, 0)),
                pl.BlockSpec((k, tn), lambda i, j: (0, j)),
                pl.BlockSpec((tm, tn), lambda i, j: (i, j)),
                pl.BlockSpec((SUBLANES, tn), lambda i, j: (0, gt_chunk * n_j + j)),
                pl.BlockSpec((1, d), lambda i, j: (0, 0)),
                pl.BlockSpec((1, d), lambda i, j: (0, 0))]
    args = [a, w, x, mod, ln_g.reshape(1, d), ln_b.reshape(1, d)]
    row_spec = pl.BlockSpec((tm, d), lambda i, j: (i, 0))
    if next_mod_chunks is None:
        kernel = functools.partial(_proj_ln_kernel_noh, **kw)
        out_specs = row_spec
        out_shape = jax.ShapeDtypeStruct((m, d), F32)
    else:
        sh_chunk, sc_chunk = next_mod_chunks
        kernel = functools.partial(_proj_ln_kernel, **kw)
        in_specs += [pl.BlockSpec((SUBLANES, d), lambda i, j: (0, sc_chunk)),
                     pl.BlockSpec((SUBLANES, d), lambda i, j: (0, sh_chunk))]
        args += [mod, mod]
        out_specs = [row_spec, row_spec]
        out_shape = [jax.ShapeDtypeStruct((m, d), F32), jax.ShapeDtypeStruct((m, d), BF16)]
    return pl.pallas_call(
        kernel,
        grid=(m // tm, n_j),
        in_specs=in_specs,
        out_specs=out_specs,
        out_shape=out_shape,
        compiler_params=_params("parallel", "arbitrary"),
        name=name,
    )(*args)


def _ffn_up_kernel(h_ref, wg_ref, wu_ref, o_ref):
    h = h_ref[...]
    g = jnp.dot(h, wg_ref[...], preferred_element_type=F32)
    u = jnp.dot(h, wu_ref[...], preferred_element_type=F32)
    o_ref[...] = (_silu(g) * u).astype(BF16)


def _ffn_up(h, wg, wu):
    m, d = h.shape
    f = wg.shape[1]
    tm = _tile(m, 1024, ROW_BLOCK)
    tn = _tile(f, 512, LANES)
    return pl.pallas_call(
        _ffn_up_kernel,
        grid=(m // tm, f // tn),
        in_specs=[pl.BlockSpec((tm, d), lambda i, j: (i, 0)),
                  pl.BlockSpec((d, tn), lambda i, j: (0, j)),
                  pl.BlockSpec((d, tn), lambda i, j: (0, j))],
        out_specs=pl.BlockSpec((tm, tn), lambda i, j: (i, j)),
        out_shape=jax.ShapeDtypeStruct((m, f), BF16),
        compiler_params=_params("parallel", "arbitrary"),
        name="ffn_up",
    )(h, wg, wu)


def _rope_tables(s, c):
    quarter = DIFF_DQK // 4
    inv_freq = ROPE_BASE ** (-jnp.arange(quarter, dtype=F32) / quarter)
    pos = jnp.arange(s, dtype=jnp.int32)
    lane = jnp.arange(LANES)
    use_col = (lane % DIFF_DQK) >= (DIFF_DQK // 2)
    p = jnp.where(use_col[None, :], (pos % GRID_W)[:, None], (pos // GRID_W)[:, None]).astype(F32)
    ang = p * inv_freq[lane % quarter][None, :]
    sign = jnp.where((lane % (2 * quarter)) < quarter, -1.0, 1.0)[None, :]
    cos = jnp.concatenate([jnp.cos(ang), jnp.ones((c, LANES), F32)], axis=0)
    sin = jnp.concatenate([jnp.sin(ang) * sign, jnp.zeros((c, LANES), F32)], axis=0)
    return cos, sin


def kernel(x, c, ctx, c_ctx, w_ada, b_ada, w_in, conv_w, a_log, dt_bias, gdn_norm_w, lam_q1, lam_k1,
           lam_q2, lam_k2, diff_norm_w, w_proj_a, w_proj_b, w_out, ln1_g, ln1_b, w_gate, w_up, w_down,
           ln2_g, ln2_b):
    bsz, s, d = x.shape
    cl = ctx.shape[1]
    t = s + cl
    nh = d // HEAD_W
    qk_w = nh * GDN_DK
    v_w = nh * GDN_DV
    dqk_w = nh * 2 * DIFF_DQK
    dv_w = nh * DIFF_DV
    alpha = (2.0 * w_ada.shape[0]) ** 0.25
    assert w_ada.shape[0] == 1 and s % ROW_BLOCK == 0 and cl % ROW_BLOCK == 0 and 8 * nh <= LANES

    c8 = jnp.zeros((SUBLANES, d), F32).at[:bsz].set(c).at[bsz].set(c_ctx)
    mod = _ada(c8, w_ada[0], b_ada[0])

    h_all = _modulate(x, ctx, mod, bsz)

    w_l = jnp.swapaxes(w_in[0], 0, 1)
    z_col = 2 * qk_w + v_w
    ab_start = z_col + v_w
    d_start = ab_start + 4 * nh
    n_d = 2 * dqk_w + dv_w + 2 * d
    h2d = h_all.reshape(bsz * t, d)
    tm_in = _tile(bsz * t, 1088, 2 * SUBLANES)
    p_g = _in_proj(h2d, w_l, 0, ab_start, tm=tm_in, tn=_tile(ab_start, 512, LANES),
                   name="in_proj_gdn").reshape(bsz, t, ab_start)
    p_ab = _in_proj(h2d, w_l, ab_start, LANES, tm=tm_in, tn=LANES,
                    name="in_proj_ab").reshape(bsz, t, LANES)
    p_d = _in_proj(h2d, w_l, d_start, n_d, tm=tm_in,
                   tn=_tile(math.gcd(n_d, d_start - d_start % LANES), 512, LANES),
                   name="in_proj_diff").reshape(bsz, t, n_d)
    ga_col = 2 * dqk_w + dv_w
    gb_col = ga_col + d

    w2 = 2 * nh
    alog128 = jnp.pad(a_log[0].reshape(1, w2), ((0, 0), (0, LANES - w2)))
    dtb128 = jnp.pad(dt_bias[0].reshape(1, w2), ((0, 0), (0, LANES - w2)))
    gb = _gates(p_ab, alog128, dtb128, w2)
    gr = jnp.swapaxes(gb[:, :, :w2], 1, 2)
    g_tot = gb[:, ::GDN_CHUNK, 3 * w2:4 * w2].reshape(-1)
    qkv = _gdn_conv(p_g, conv_w[0], nh, s)
    u, w_, qd, qk, kdt = _gdn_intra(qkv, gb, gr, nh)
    o_f, o_b = _gdn_scan(g_tot, u, w_, qd, qk, kdt, nh, s)
    y_a = _gdn_out(o_f, o_b, p_g, z_col // v_w, gdn_norm_w[0], nh, s)

    cos_t, sin_t = _rope_tables(s, cl)
    qr, kr, vr = _rope(p_d, cos_t, sin_t, nh)
    lam8 = jnp.zeros((SUBLANES, LANES), F32)
    for r, vec in enumerate((lam_q1, lam_k1, lam_q2, lam_k2)):
        lam8 = lam8.at[r, :DIFF_DQK].set(vec[0])
    y_b = _attention(qr, kr, vr, lam8, diff_norm_w[0], s)

    ymg = _merge(y_a, y_b, w_proj_a[0].astype(BF16), w_proj_b[0].astype(BF16), p_d, ga_col, gb_col)
    m = bsz * s
    x1, h2 = _proj_ln(ymg.reshape(m, d), w_out[0].astype(BF16), x.reshape(m, d), mod, 2,
                      ln1_g[0], ln1_b[0], s, alpha=alpha, tm=_tile(s, 512, ROW_BLOCK),
                      tn=_tile(d, 512, LANES), name="out_proj_ln", next_mod_chunks=(3, 4))

    hff = _ffn_up(h2, w_gate[0].astype(BF16), w_up[0].astype(BF16))
    out = _proj_ln(hff, w_down[0].astype(BF16), x1, mod, 5, ln2_g[0], ln2_b[0], s, alpha=alpha,
                   tm=_tile(s, 512, ROW_BLOCK), tn=_tile(d, 256, LANES), name="ffn_down_ln")
    return out.reshape(bsz, s, d)
```

```python
import functools
import math

import jax
import jax.numpy as jnp
from jax import lax
from jax.experimental import pallas as pl
from jax.experimental.pallas import tpu as pltpu

F32 = jnp.float32
BF16 = jnp.bfloat16

GRID_W = 64
HEAD_W = 256
GDN_DK = 128
GDN_DV = 128
GDN_CHUNK = 64
DIFF_DQK = 64
DIFF_DV = 128
ROPE_BASE = 10000.0
LN_EPS = 1e-6
LAM_INIT = 0.8 - 0.6 * math.exp(-0.3 * 0)
Q_SCALE = DIFF_DQK ** -0.5 * math.log2(math.e)

LANES = 128
SUBLANES = 8
PAIR = 2 * GDN_CHUNK
ROW_BLOCK = 256
VMEM_LIMIT = 56 * 1024 * 1024

_NT = (((1,), (1,)), ((), ()))


def _params(*sem):
    return pltpu.CompilerParams(dimension_semantics=sem, vmem_limit_bytes=VMEM_LIMIT)


def _tile(n, target, quantum):
    best = None
    t = quantum
    while t <= min(n, target):
        if n % t == 0:
            best = t
        t += quantum
    assert best is not None, (n, target, quantum)
    return best


def _silu(x):
    return x * jax.nn.sigmoid(x)


def _ada_kernel(c_ref, w_ref, b_ref, o_ref):
    c = _silu(c_ref[...])
    o_ref[...] = jnp.dot(c.astype(BF16), w_ref[...].astype(BF16),
                         preferred_element_type=F32) + b_ref[...]


def _ada(c8, w_ada, b_ada):
    d, n = w_ada.shape
    tn = _tile(n, 512, LANES)
    return pl.pallas_call(
        _ada_kernel,
        grid=(n // tn,),
        in_specs=[pl.BlockSpec((SUBLANES, d), lambda j: (0, 0)),
                  pl.BlockSpec((d, tn), lambda j: (0, j)),
                  pl.BlockSpec((1, tn), lambda j: (0, j))],
        out_specs=pl.BlockSpec((SUBLANES, tn), lambda j: (0, j)),
        out_shape=jax.ShapeDtypeStruct((SUBLANES, n), F32),
        compiler_params=_params("parallel"),
        name="ada",
    )(c8, w_ada, b_ada.reshape(1, n))


def _modulate_kernel(x_ref, ctx_ref, sh_ref, sc_ref, o_ref, *, n_lat, ctx_row):
    b = pl.program_id(0)
    i = pl.program_id(1)

    @pl.when(i < n_lat)
    def _():
        sc = sc_ref[pl.ds(b, 1), :]
        sh = sh_ref[pl.ds(b, 1), :]
        o_ref[0] = (x_ref[0] * (1.0 + sc) + sh).astype(BF16)

    @pl.when(i >= n_lat)
    def _():
        sc = sc_ref[ctx_row:ctx_row + 1, :]
        sh = sh_ref[ctx_row:ctx_row + 1, :]
        o_ref[0] = (ctx_ref[0] * (1.0 + sc) + sh).astype(BF16)


def _modulate(x, ctx, mod, ctx_row):
    bsz, s, d = x.shape
    c = ctx.shape[1]
    t = s + c
    rb = ROW_BLOCK
    n_lat = s // rb
    return pl.pallas_call(
        functools.partial(_modulate_kernel, n_lat=n_lat, ctx_row=ctx_row),
        grid=(bsz, t // rb),
        in_specs=[pl.BlockSpec((1, rb, d), lambda b, i: (b, jnp.minimum(i, n_lat - 1), 0)),
                  pl.BlockSpec((1, rb, d), lambda b, i: (b, jnp.maximum(i - n_lat, 0), 0)),
                  pl.BlockSpec((SUBLANES, d), lambda b, i: (0, 0)),
                  pl.BlockSpec((SUBLANES, d), lambda b, i: (0, 1))],
        out_specs=pl.BlockSpec((1, rb, d), lambda b, i: (b, i, 0)),
        out_shape=jax.ShapeDtypeStruct((bsz, t, d), BF16),
        compiler_params=_params("parallel", "arbitrary"),
        name="modulate",
    )(x, ctx, mod, mod)


def _in_proj_kernel(a_ref, w_hbm, o_ref, wb_ref, stage_ref, sem, *, row0, n_tiles, n_chunks):
    j = pl.program_id(0)
    i = pl.program_id(1)
    tn = wb_ref.shape[1]
    ch = tn // n_chunks

    def chunk_copy(tile, c, slot):
        rows = pl.ds(pl.multiple_of(row0 + tile * tn + c * ch, SUBLANES), ch)
        return pltpu.make_async_copy(w_hbm.at[rows, :], stage_ref.at[slot], sem.at[slot])

    @pl.when(jnp.logical_and(j == 0, i == 0))
    def _():
        chunk_copy(0, 0, 0).start()
        for c in range(n_chunks):
            if c + 1 < n_chunks:
                chunk_copy(0, c + 1, (c + 1) % 2).start()
            chunk_copy(0, c, c % 2).wait()
            wb_ref[0, c * ch:(c + 1) * ch, :] = stage_ref[c % 2].astype(BF16)

    has_next = j + 1 < n_tiles

    @pl.when(jnp.logical_and(has_next, i < n_chunks))
    def _():
        chunk_copy(j + 1, i, i % 2).start()

    @pl.when(jnp.logical_and(has_next, jnp.logical_and(i >= 1, i <= n_chunks)))
    def _():
        c = i - 1
        chunk_copy(j + 1, c, c % 2).wait()
        wb_ref[(j + 1) % 2, pl.ds(pl.multiple_of(c * ch, ch), ch), :] = stage_ref[c % 2].astype(BF16)

    o_ref[...] = lax.dot_general(a_ref[...], wb_ref[j % 2], _NT, preferred_element_type=F32)


def _in_proj(a, w_t, row0, n, *, tm, tn, name):
    m, k = a.shape
    n_tiles = n // tn
    n_row_tiles = m // tm
    assert n % tn == 0 and row0 % SUBLANES == 0 and n_row_tiles >= 2
    n_chunks = max(c for c in range(1, tn // LANES + 1)
                   if (tn // LANES) % c == 0 and c <= n_row_tiles - 1)
    return pl.pallas_call(
        functools.partial(_in_proj_kernel, row0=row0, n_tiles=n_tiles, n_chunks=n_chunks),
        grid=(n_tiles, n_row_tiles),
        in_specs=[pl.BlockSpec((tm, k), lambda j, i: (i, 0)),
                  pl.BlockSpec(memory_space=pl.ANY)],
        out_specs=pl.BlockSpec((tm, tn), lambda j, i: (i, j)),
        out_shape=jax.ShapeDtypeStruct((m, n), F32),
        scratch_shapes=[pltpu.VMEM((2, tn, k), BF16),
                        pltpu.VMEM((2, tn // n_chunks, k), F32),
                        pltpu.SemaphoreType.DMA((2,))],
        compiler_params=_params("arbitrary", "arbitrary"),
        name=name,
    )(a, w_t)


def _gates_kernel(p_ref, alog_ref, dtb_ref, o_ref, *, w2):
    p = p_ref[0]
    rb = p.shape[0]
    lane = lax.broadcasted_iota(jnp.int32, p.shape, 1)
    row = lax.broadcasted_iota(jnp.int32, p.shape, 0) % GDN_CHUNK
    xa = p + dtb_ref[...]
    softplus = jnp.maximum(xa, 0.0) + jnp.log(1.0 + jnp.exp(-jnp.abs(xa)))
    g = jnp.where(lane < w2, -jnp.exp(alog_ref[...]) * softplus, 0.0)
    pre = g
    suf = g
    step = 1
    while step < GDN_CHUNK:
        pre = pre + jnp.where(row >= step, pltpu.roll(pre, step, axis=0), 0.0)
        suf = suf + jnp.where(row < GDN_CHUNK - step, pltpu.roll(suf, rb - step, axis=0), 0.0)
        step *= 2
    tot = pre + suf - g
    cum = jnp.where(lane < w2 // 2, pre, suf)
    beta = jax.nn.sigmoid(p)
    out = jnp.where(lane < w2, cum,
                    jnp.where(lane < 2 * w2, beta,
                              jnp.where(lane < 3 * w2, pltpu.roll(tot, 2 * w2, axis=1),
                                        pltpu.roll(jnp.exp(tot), 3 * w2, axis=1))))
    o_ref[0] = jnp.where(lane < 4 * w2, out, 0.0)


def _gates(p_ab, alog128, dtb128, w2):
    bsz, t, _ = p_ab.shape
    rb = ROW_BLOCK
    return pl.pallas_call(
        functools.partial(_gates_kernel, w2=w2),
        grid=(bsz, t // rb),
        in_specs=[pl.BlockSpec((1, rb, LANES), lambda b, i: (b, i, 0)),
                  pl.BlockSpec((1, LANES), lambda b, i: (0, 0)),
                  pl.BlockSpec((1, LANES), lambda b, i: (0, 0))],
        out_specs=pl.BlockSpec((1, rb, LANES), lambda b, i: (b, i, 0)),
        out_shape=jax.ShapeDtypeStruct((bsz, t, LANES), F32),
        compiler_params=_params("parallel", "parallel"),
        name="gdn_gates",
    )(p_ab, alog128, dtb128)


def _conv_kernel(x_ref, cw_ref, o_ref, *, n_heads, cb, n_lat, n_blk, taps):
    j = pl.program_id(1)
    t = x_ref.shape[1]
    width = x_ref.shape[2]
    rb = ROW_BLOCK
    halo = SUBLANES
    pad = taps // 2
    win_rows = rb + 2 * halo
    w = cw_ref[...]
    is_qk = j * cb < 2 * n_heads
    q_scale = jnp.where(j * cb < n_heads, GDN_DK ** -0.5, 1.0)

    def body(r, carry):
        base = pl.multiple_of(r * rb, rb)
        top_ok = jnp.logical_and(r != 0, r != n_lat)
        bot_ok = jnp.logical_and(r != n_lat - 1, r != n_blk - 1)
        top = x_ref[0, pl.ds(pl.multiple_of(jnp.maximum(base - halo, 0), halo), halo), :]
        bot = x_ref[0, pl.ds(pl.multiple_of(jnp.minimum(base + rb, t - halo), halo), halo), :]
        win = jnp.concatenate([jnp.where(top_ok, top, 0.0), x_ref[0, pl.ds(base, rb), :],
                               jnp.where(bot_ok, bot, 0.0)], axis=0)
        acc = jnp.zeros((win_rows, width), F32)
        for tap in range(taps):
            shift = (pad - tap) % win_rows
            shifted = win if shift == 0 else pltpu.roll(win, shift, axis=0)
            acc = acc + shifted * w[tap:tap + 1, :]
        y = _silu(acc[halo:halo + rb, :])
        for c in range(cb):
            yc = y[:, c * LANES:(c + 1) * LANES]
            ss = jnp.sum(yc * yc, axis=-1, keepdims=True)
            scale = jnp.where(is_qk, lax.rsqrt(ss + 1e-6) * q_scale, 1.0)
            o_ref[0, c, pl.ds(base, rb), :] = yc * scale
        return carry

    lax.fori_loop(0, n_blk, body, 0)


def _gdn_conv(p_g, conv_w, n_heads, s):
    bsz, t, _ = p_g.shape
    taps = conv_w.shape[0]
    n_cols = 3 * n_heads
    cb = _tile(n_heads, 2, 1)
    return pl.pallas_call(
        functools.partial(_conv_kernel, n_heads=n_heads, cb=cb, n_lat=s // ROW_BLOCK,
                          n_blk=t // ROW_BLOCK, taps=taps),
        grid=(bsz, n_cols // cb),
        in_specs=[pl.BlockSpec((1, t, cb * LANES), lambda b, j: (b, 0, j)),
                  pl.BlockSpec((taps, cb * LANES), lambda b, j: (0, j))],
        out_specs=pl.BlockSpec((1, cb, t, LANES), lambda b, j: (b, j, 0, 0)),
        out_shape=jax.ShapeDtypeStruct((bsz, n_cols, t, LANES), F32),
        compiler_params=_params("parallel", "parallel"),
        name="gdn_conv",
    )(p_g, conv_w)


def _gdn_intra_kernel(q_ref, k_ref, v_ref, gb_ref, gr_ref,
                      u_ref, w_ref, qd_ref, qk_ref, kdt_ref, *, n_heads, hb):
    hblk = pl.program_id(1)
    w2 = 2 * n_heads
    rb = q_ref.shape[2]
    gbv = gb_ref[0]
    lane = lax.broadcasted_iota(jnp.int32, gbv.shape, 1)

    def column(idx):
        return jnp.sum(jnp.where(lane == idx, gbv, 0.0), axis=-1, keepdims=True)

    ri = lax.broadcasted_iota(jnp.int32, (PAIR, PAIR), 0)
    ci = lax.broadcasted_iota(jnp.int32, (PAIR, PAIR), 1)
    same = (ri >= GDN_CHUNK) == (ci >= GDN_CHUNK)
    eye_f32 = jnp.where(ri == ci, 1.0, 0.0)
    levels = int(math.log2(GDN_CHUNK))
    off_blocks = ([], [])
    for lvl in range(levels):
        half = 1 << lvl
        same_blk = (ri >> (lvl + 1)) == (ci >> (lvl + 1))
        r_hi = (ri & half) != 0
        c_hi = (ci & half) != 0
        off_blocks[0].append(same_blk & r_hi & jnp.logical_not(c_hi))
        off_blocks[1].append(same_blk & c_hi & jnp.logical_not(r_hi))

    incls = (jnp.logical_and(same, ri >= ci), jnp.logical_and(same, ri <= ci))
    stricts = (jnp.logical_and(same, ri > ci), jnp.logical_and(same, ri < ci))

    probs = []
    for hh in range(hb):
        head = hblk * hb + hh
        gcols = [column(d * n_heads + head) for d in range(2)]
        bcols = [column(w2 + d * n_heads + head) for d in range(2)]
        tcols = [column(2 * w2 + d * n_heads + head) for d in range(2)]
        grows = [gr_ref[0, pl.ds(d * n_heads + head, 1), :] for d in range(2)]
        for p in range(rb // PAIR):
            rows = slice(p * PAIR, (p + 1) * PAIR)
            q = q_ref[0, hh, rows, :]
            k = k_ref[0, hh, rows, :]
            kbf = k.astype(BF16)
            kk = lax.dot_general(kbf, kbf, _NT, preferred_element_type=F32)
            qkm = lax.dot_general(q.astype(BF16), kbf, _NT, preferred_element_type=F32)
            for d in range(2):
                gc = gcols[d][rows]
                bc = bcols[d][rows]
                dec = jnp.where(incls[d], jnp.exp(jnp.where(incls[d], gc - grows[d][:, rows], 0.0)), 0.0)
                a = jnp.where(stricts[d], kk * bc * dec, 0.0)
                qk_ref[0, d, hh, rows, :] = jnp.where(incls[d], qkm * dec, 0.0).astype(BF16)
                probs.append(dict(hh=hh, rows=rows, d=d, a=a, gc=gc, bc=bc, tc=tcols[d][rows]))

    minv = [eye_f32 - jnp.where(off_blocks[pr["d"]][0], pr["a"], 0.0) for pr in probs]
    a_bf = [pr["a"].astype(BF16) for pr in probs]
    for lvl in range(1, levels):
        mb = [m.astype(BF16) for m in minv]
        t1 = [jnp.dot(a_bf[i], mb[i], preferred_element_type=F32).astype(BF16)
              for i in range(len(probs))]
        minv = [minv[i] - jnp.where(off_blocks[pr["d"]][lvl],
                                    jnp.dot(mb[i], t1[i], preferred_element_type=F32), 0.0)
                for i, pr in enumerate(probs)]

    for i, pr in enumerate(probs):
        hh, rows, d, gc, bc = pr["hh"], pr["rows"], pr["d"], pr["gc"], pr["bc"]
        q = q_ref[0, hh, rows, :]
        k = k_ref[0, hh, rows, :]
        v = v_ref[0, hh, rows, :]
        eg = jnp.exp(gc)
        vb = v * bc
        kbg = k * (bc * eg)
        rhs = jnp.concatenate([vb, kbg], axis=1).astype(BF16)
        uw = jnp.dot((minv[i] - eye_f32).astype(BF16), rhs, preferred_element_type=F32)
        u_ref[0, d, hh, rows, :] = vb + uw[:, :GDN_DV]
        w_ref[0, d, hh, rows, :] = (kbg + uw[:, GDN_DV:]).astype(BF16)
        qd_ref[0, d, hh, rows, :] = (q * eg).astype(BF16)
        kdt_ref[0, d, hh, rows, :] = (k * jnp.exp(pr["tc"] - gc)).T.astype(BF16)


def _gdn_intra(qkv, gb, gr, n_heads):
    bsz, _, t, _ = qkv.shape
    rb = ROW_BLOCK
    nh = n_heads
    hb = _tile(nh, 4, 1)
    nhb = nh // hb

    def qkv_spec(part):
        return pl.BlockSpec((1, hb, rb, LANES), lambda b, h, i: (b, part * nhb + h, i, 0))

    out_spec = pl.BlockSpec((1, 2, hb, rb, LANES), lambda b, h, i: (b, 0, h, i, 0))
    shape = (bsz, 2, nh, t, LANES)
    return pl.pallas_call(
        functools.partial(_gdn_intra_kernel, n_heads=nh, hb=hb),
        grid=(bsz, nhb, t // rb),
        in_specs=[qkv_spec(0), qkv_spec(1), qkv_spec(2),
                  pl.BlockSpec((1, rb, LANES), lambda b, h, i: (b, i, 0)),
                  pl.BlockSpec((1, gr.shape[1], rb), lambda b, h, i: (b, 0, i))],
        out_specs=[out_spec] * 5,
        out_shape=[jax.ShapeDtypeStruct(shape, F32)] + [jax.ShapeDtypeStruct(shape, BF16)] * 4,
        compiler_params=_params("parallel", "parallel", "parallel"),
        name="gdn_intra",
    )(qkv, qkv, qkv, gb, gr)


def _gdn_scan_kernel(gt_ref, uf, wf, qdf, qkf, kdtf, ub, wb, qdb, qkb, kdtb,
                     of_ref, ob_ref, s_ref, *, n_heads, hb, n_lat_blk, n_blk):
    b = pl.program_id(0)
    hblk = pl.program_id(1)
    i = pl.program_id(2)
    rb = uf.shape[3]
    n_ch = rb // GDN_CHUNK
    w2 = 2 * n_heads

    @pl.when(i == 0)
    def _():
        s_ref[...] = jnp.zeros(s_ref.shape, F32)

    blk_f = (i + n_lat_blk) % n_blk
    blk_b = n_blk - 1 - i
    zeros = jnp.zeros((GDN_CHUNK, GDN_DV), BF16)
    dirs = ((0, uf, wf, qdf, qkf, kdtf, of_ref, blk_f),
            (1, ub, wb, qdb, qkb, kdtb, ob_ref, blk_b))
    for c in range(n_ch):
        chains = []
        for d, u_r, w_r, qd_r, qk_r, kdt_r, o_r, blk in dirs:
            cc = c if d == 0 else n_ch - 1 - c
            rows = slice(cc * GDN_CHUNK, (cc + 1) * GDN_CHUNK)
            pair_rows = slice((cc // 2) * PAIR, (cc // 2 + 1) * PAIR)
            for hh in range(hb):
                g_tot = gt_ref[(b * (n_blk * n_ch) + blk * n_ch + cc) * w2 + d * n_heads + hblk * hb + hh]
                chains.append((d, hh, cc, rows, pair_rows, u_r, w_r, qd_r, qk_r, kdt_r, o_r, g_tot))
        states = [s_ref[d, hh] for d, hh, *_ in chains]
        m1s = [jnp.dot(jnp.concatenate([w_r[0, 0, hh, rows, :], qd_r[0, 0, hh, rows, :]], axis=0),
                       states[n].astype(BF16), preferred_element_type=F32)
               for n, (d, hh, cc, rows, pair_rows, u_r, w_r, qd_r, qk_r, kdt_r, o_r, g_tot) in enumerate(chains)]
        m2s = []
        for n, (d, hh, cc, rows, pair_rows, u_r, w_r, qd_r, qk_r, kdt_r, o_r, g_tot) in enumerate(chains):
            v_new = (u_r[0, 0, hh, rows, :] - m1s[n][:GDN_CHUNK]).astype(BF16)
            v_ext = (jnp.concatenate([v_new, zeros], axis=0) if cc % 2 == 0
                     else jnp.concatenate([zeros, v_new], axis=0))
            lhs2 = jnp.concatenate([qk_r[0, 0, hh, rows, :], kdt_r[0, 0, hh, pair_rows, :]], axis=0)
            m2s.append(jnp.dot(lhs2, v_ext, preferred_element_type=F32))
        for n, (d, hh, cc, rows, pair_rows, u_r, w_r, qd_r, qk_r, kdt_r, o_r, g_tot) in enumerate(chains):
            s_ref[d, hh] = states[n] * g_tot + m2s[n][GDN_CHUNK:]
            o_r[0, rows, hh * GDN_DV:(hh + 1) * GDN_DV] = m1s[n][GDN_CHUNK:] + m2s[n][:GDN_CHUNK]


def _gdn_scan(g_tot, u, w, qd, qk, kdt, n_heads, s):
    bsz, _, nh, t, _ = u.shape
    rb = ROW_BLOCK
    hb = _tile(nh, 8, 1)
    n_blk = t // rb
    n_lat_blk = s // rb

    def spec(d):
        if d == 0:
            return pl.BlockSpec((1, 1, hb, rb, LANES),
                                lambda b, h, i: (b, 0, h, (i + n_lat_blk) % n_blk, 0))
        return pl.BlockSpec((1, 1, hb, rb, LANES), lambda b, h, i: (b, 1, h, n_blk - 1 - i, 0))

    o_shape = jax.ShapeDtypeStruct((bsz, t, nh * GDN_DV), F32)
    return pl.pallas_call(
        functools.partial(_gdn_scan_kernel, n_heads=nh, hb=hb, n_lat_blk=n_lat_blk, n_blk=n_blk),
        grid=(bsz, nh // hb, n_blk),
        in_specs=[pl.BlockSpec(memory_space=pltpu.SMEM)] + [spec(0)] * 5 + [spec(1)] * 5,
        out_specs=[pl.BlockSpec((1, rb, hb * GDN_DV), lambda b, h, i: (b, (i + n_lat_blk) % n_blk, h)),
                   pl.BlockSpec((1, rb, hb * GDN_DV), lambda b, h, i: (b, n_blk - 1 - i, h))],
        out_shape=[o_shape, o_shape],
        scratch_shapes=[pltpu.VMEM((2, hb, GDN_DK, GDN_DV), F32)],
        compiler_params=_params("parallel", "parallel", "arbitrary"),
        name="gdn_scan",
    )(g_tot, u, w, qd, qk, kdt, u, w, qd, qk, kdt)


def _gdn_out_kernel(of_ref, ob_ref, z_ref, nw_ref, o_ref, *, n_heads):
    nw = nw_ref[...]
    for h in range(n_heads):
        cols = slice(h * GDN_DV, (h + 1) * GDN_DV)
        o = of_ref[0, :, cols] + ob_ref[0, :, cols]
        ms = jnp.mean(o * o, axis=-1, keepdims=True)
        o_ref[0, :, cols] = (o * lax.rsqrt(ms + 1e-6) * nw * _silu(z_ref[0, :, cols])).astype(BF16)


def _gdn_out(o_f, o_b, p3, z_col_blk, gdn_norm_w, n_heads, s):
    bsz = o_f.shape[0]
    width = n_heads * GDN_DV
    tm = ROW_BLOCK
    return pl.pallas_call(
        functools.partial(_gdn_out_kernel, n_heads=n_heads),
        grid=(bsz, s // tm),
        in_specs=[pl.BlockSpec((1, tm, width), lambda b, i: (b, i, 0)),
                  pl.BlockSpec((1, tm, width), lambda b, i: (b, i, 0)),
                  pl.BlockSpec((1, tm, width), lambda b, i: (b, i, z_col_blk)),
                  pl.BlockSpec((1, GDN_DV), lambda b, i: (0, 0))],
        out_specs=pl.BlockSpec((1, tm, width), lambda b, i: (b, i, 0)),
        out_shape=jax.ShapeDtypeStruct((bsz, s, width), BF16),
        compiler_params=_params("parallel", "parallel"),
        name="gdn_out",
    )(o_f, o_b, p3, gdn_norm_w.reshape(1, GDN_DV))


def _rope_kernel(q_ref, k_ref, v_ref, cos_ref, sin_ref, qo_ref, ko_ref, vo_ref, *, n_heads):
    cos = cos_ref[...]
    sin = sin_ref[...]
    lane = lax.broadcasted_iota(jnp.int32, cos.shape, 1)
    first_half = (lane % (DIFF_DQK // 2)) < (DIFF_DQK // 4)
    quarter = DIFF_DQK // 4

    def rot(x):
        partner = jnp.where(first_half, pltpu.roll(x, LANES - quarter, axis=1),
                            pltpu.roll(x, quarter, axis=1))
        return x * cos + partner * sin

    for h in range(n_heads):
        cols = slice(h * LANES, (h + 1) * LANES)
        qo_ref[0, h] = (rot(q_ref[0, :, cols]) * Q_SCALE).astype(BF16)
        ko_ref[0, h] = rot(k_ref[0, :, cols]).astype(BF16)
        vo_ref[0, h] = v_ref[0, :, cols].astype(BF16)


def _rope(p_d, cos_t, sin_t, n_heads):
    bsz, t, _ = p_d.shape
    rb = ROW_BLOCK
    width = n_heads * LANES

    def in_spec(part):
        return pl.BlockSpec((1, rb, width), lambda b, i: (b, i, part))

    tab = pl.BlockSpec((rb, LANES), lambda b, i: (i, 0))
    out = pl.BlockSpec((1, n_heads, rb, LANES), lambda b, i: (b, 0, i, 0))
    shape = jax.ShapeDtypeStruct((bsz, n_heads, t, LANES), BF16)
    return pl.pallas_call(
        functools.partial(_rope_kernel, n_heads=n_heads),
        grid=(bsz, t // rb),
        in_specs=[in_spec(0), in_spec(1), in_spec(2), tab, tab],
        out_specs=[out, out, out],
        out_shape=[shape, shape, shape],
        compiler_params=_params("parallel", "parallel"),
        name="diff_rope",
    )(p_d, p_d, p_d, cos_t, sin_t)


def _attn_kernel(q_ref, k_ref, v_ref, lam_ref, nw_ref, o_ref, *, key_chunks):
    q = q_ref[0, 0]
    tq = q.shape[0]
    lane = lax.broadcasted_iota(jnp.int32, q.shape, 1)
    q2 = jnp.concatenate([jnp.where(lane < DIFF_DQK, q, jnp.zeros_like(q)),
                          jnp.where(lane >= DIFF_DQK, q, jnp.zeros_like(q))], axis=0)
    lp = lam_ref[...]
    lam = (jnp.exp(jnp.sum(lp[0:1] * lp[1:2], axis=-1, keepdims=True))
           - jnp.exp(jnp.sum(lp[2:3] * lp[3:4], axis=-1, keepdims=True)) + LAM_INIT)
    m = jnp.full((2 * tq, 1), -jnp.inf, F32)
    l = jnp.zeros((2 * tq, 1), F32)
    acc = jnp.zeros((2 * tq, DIFF_DV), F32)
    for start, size in key_chunks:
        k = k_ref[0, 0, start:start + size, :]
        v = v_ref[0, 0, start:start + size, :]
        s = lax.dot_general(q2, k, _NT, preferred_element_type=F32)
        m_new = jnp.maximum(m, jnp.max(s, axis=-1, keepdims=True))
        alpha = jnp.exp2(m - m_new)
        e = jnp.exp2(s - m_new)
        l = alpha * l + jnp.sum(e, axis=-1, keepdims=True)
        acc = alpha * acc + jnp.dot(e.astype(BF16), v, preferred_element_type=F32)
        m = m_new
    o = acc[:tq] / l[:tq] - lam * (acc[tq:] / l[tq:])
    ms = jnp.mean(o * o, axis=-1, keepdims=True)
    o_ref[0] = (o * lax.rsqrt(ms + 1e-5) * nw_ref[...] * (1.0 - LAM_INIT)).astype(BF16)


def _attention(qr, kr, vr, lam8, diff_norm_w, s):
    bsz, nh, t, _ = kr.shape
    tq = _tile(s, 512, ROW_BLOCK)
    kc = _tile(s, 2048, ROW_BLOCK)
    key_chunks = [(st, kc) for st in range(0, s, kc)] + [(s, t - s)]
    return pl.pallas_call(
        functools.partial(_attn_kernel, key_chunks=tuple(key_chunks)),
        grid=(bsz, nh, s // tq),
        in_specs=[pl.BlockSpec((1, 1, tq, LANES), lambda b, h, i: (b, h, i, 0)),
                  pl.BlockSpec((1, 1, t, LANES), lambda b, h, i: (b, h, 0, 0)),
                  pl.BlockSpec((1, 1, t, LANES), lambda b, h, i: (b, h, 0, 0)),
                  pl.BlockSpec((SUBLANES, LANES), lambda b, h, i: (0, 0)),
                  pl.BlockSpec((1, DIFF_DV), lambda b, h, i: (0, 0))],
        out_specs=pl.BlockSpec((1, tq, DIFF_DV), lambda b, h, i: (b, i, h)),
        out_shape=jax.ShapeDtypeStruct((bsz, s, nh * DIFF_DV), BF16),
        compiler_params=_params("parallel", "parallel", "arbitrary"),
        name="diff_attn",
    )(qr, kr, vr, lam8, diff_norm_w.reshape(1, DIFF_DV))


def _merge_kernel(ya_ref, yb_ref, wa_ref, wb_ref, ga_ref, gb_ref, o_ref):
    pa = jnp.dot(ya_ref[0], wa_ref[...], preferred_element_type=F32)
    pb = jnp.dot(yb_ref[0], wb_ref[...], preferred_element_type=F32)
    o_ref[0] = (jax.nn.sigmoid(ga_ref[0]) * pa + jax.nn.sigmoid(gb_ref[0]) * pb).astype(BF16)


def _merge(y_a, y_b, wa, wb, p3, ga_col, gb_col):
    bsz, s, ka = y_a.shape
    kb = y_b.shape[2]
    d = wa.shape[1]
    tm = _tile(s, 512, ROW_BLOCK)
    tn = _tile(math.gcd(d, ga_col, gb_col), 1024, LANES)
    return pl.pallas_call(
        _merge_kernel,
        grid=(bsz, s // tm, d // tn),
        in_specs=[pl.BlockSpec((1, tm, ka), lambda b, i, j: (b, i, 0)),
                  pl.BlockSpec((1, tm, kb), lambda b, i, j: (b, i, 0)),
                  pl.BlockSpec((ka, tn), lambda b, i, j: (0, j)),
                  pl.BlockSpec((kb, tn), lambda b, i, j: (0, j)),
                  pl.BlockSpec((1, tm, tn), lambda b, i, j: (b, i, ga_col // tn + j)),
                  pl.BlockSpec((1, tm, tn), lambda b, i, j: (b, i, gb_col // tn + j))],
        out_specs=pl.BlockSpec((1, tm, tn), lambda b, i, j: (b, i, j)),
        out_shape=jax.ShapeDtypeStruct((bsz, s, d), BF16),
        compiler_params=_params("parallel", "parallel", "arbitrary"),
        name="merge",
    )(y_a, y_b, wa, wb, p3, p3)


def _layernorm_rows(ref, g, bvec, emit, rows_per_step=32):
    tm = ref.shape[0]

    def body(r, carry):
        rows = pl.ds(pl.multiple_of(r * rows_per_step, rows_per_step), rows_per_step)
        x = ref[rows, :]
        mu = jnp.mean(x, axis=-1, keepdims=True)
        xc = x - mu
        var = jnp.mean(xc * xc, axis=-1, keepdims=True)
        y = xc * lax.rsqrt(var + LN_EPS) * g + bvec
        ref[rows, :] = y
        emit(rows, y)
        return carry

    lax.fori_loop(0, tm // rows_per_step, body, 0)


def _proj_ln_kernel(a_ref, w_ref, x_ref, gt_ref, g_ref, b_ref, sc_ref, sh_ref,
                    o_ref, h_ref, *, alpha, n_j, tn, rows_per_batch):
    i = pl.program_id(0)
    j = pl.program_id(1)
    tm = a_ref.shape[0]
    b = (i * tm) // rows_per_batch
    mix = jnp.dot(a_ref[...], w_ref[...], preferred_element_type=F32)
    r = alpha * x_ref[...] + gt_ref[pl.ds(b, 1), :] * mix
    for jj in range(n_j):
        @pl.when(j == jj)
        def _(jj=jj):
            o_ref[:, jj * tn:(jj + 1) * tn] = r

    @pl.when(j == n_j - 1)
    def _():
        if h_ref is None:
            _layernorm_rows(o_ref, g_ref[...], b_ref[...], lambda rows, y: None)
        else:
            sc = 1.0 + sc_ref[pl.ds(b, 1), :]
            sh = sh_ref[pl.ds(b, 1), :]

            def emit(rows, y):
                h_ref[rows, :] = (y * sc + sh).astype(BF16)

            _layernorm_rows(o_ref, g_ref[...], b_ref[...], emit)


def _proj_ln_kernel_noh(a_ref, w_ref, x_ref, gt_ref, g_ref, b_ref, o_ref, **kw):
    _proj_ln_kernel(a_ref, w_ref, x_ref, gt_ref, g_ref, b_ref, None, None, o_ref, None, **kw)


def _proj_ln(a, w, x, mod, gt_chunk, ln_g, ln_b, rows_per_batch, *, alpha, tm, tn, name,
             next_mod_chunks=None):
    m, k = a.shape
    d = w.shape[1]
    n_j = d // tn
    kw = dict(alpha=alpha, n_j=n_j, tn=tn, rows_per_batch=rows_per_batch)
    in_specs = [pl.BlockSpec((tm, k), lambda i, j: (i, 0)),
                pl.BlockSpec((k, tn), lambda i, j: (0, j)),
                pl.BlockSpec((tm, tn), lambda i, j: (i, j)),
                pl.BlockSpec((SUBLANES, tn), lambda i, j: (0, gt_chunk * n_j + j)),
                pl.BlockSpec((1, d), lambda i, j: (0, 0)),
                pl.BlockSpec((1, d), lambda i, j: (0, 0))]
    args = [a, w, x, mod, ln_g.reshape(1, d), ln_b.reshape(1, d)]
    row_spec = pl.BlockSpec((tm, d), lambda i, j: (i, 0))
    if next_mod_chunks is None:
        kernel = functools.partial(_proj_ln_kernel_noh, **kw)
        out_specs = row_spec
        out_shape = jax.ShapeDtypeStruct((m, d), F32)
    else:
        sh_chunk, sc_chunk = next_mod_chunks
        kernel = functools.partial(_proj_ln_kernel, **kw)
        in_specs += [pl.BlockSpec((SUBLANES, d), lambda i, j: (0, sc_chunk)),
                     pl.BlockSpec((SUBLANES, d), lambda i, j: (0, sh_chunk))]
        args += [mod, mod]
        out_specs = [row_spec, row_spec]
        out_shape = [jax.ShapeDtypeStruct((m, d), F32), jax.ShapeDtypeStruct((m, d), BF16)]
    return pl.pallas_call(
        kernel,
        grid=(m // tm, n_j),
        in_specs=in_specs,
        out_specs=out_specs,
        out_shape=out_shape,
        compiler_params=_params("parallel", "arbitrary"),
        name=name,
    )(*args)


def _ffn_up_kernel(h_ref, wg_ref, wu_ref, o_ref):
    h = h_ref[...]
    g = jnp.dot(h, wg_ref[...], preferred_element_type=F32)
    u = jnp.dot(h, wu_ref[...], preferred_element_type=F32)
    o_ref[...] = (_silu(g) * u).astype(BF16)


def _ffn_up(h, wg, wu):
    m, d = h.shape
    f = wg.shape[1]
    tm = _tile(m, 1024, ROW_BLOCK)
    tn = _tile(f, 512, LANES)
    return pl.pallas_call(
        _ffn_up_kernel,
        grid=(m // tm, f // tn),
        in_specs=[pl.BlockSpec((tm, d), lambda i, j: (i, 0)),
                  pl.BlockSpec((d, tn), lambda i, j: (0, j)),
                  pl.BlockSpec((d, tn), lambda i, j: (0, j))],
        out_specs=pl.BlockSpec((tm, tn), lambda i, j: (i, j)),
        out_shape=jax.ShapeDtypeStruct((m, f), BF16),
        compiler_params=_params("parallel", "arbitrary"),
        name="ffn_up",
    )(h, wg, wu)


def _rope_tables(s, c):
    quarter = DIFF_DQK // 4
    inv_freq = ROPE_BASE ** (-jnp.arange(quarter, dtype=F32) / quarter)
    pos = jnp.arange(s, dtype=jnp.int32)
    lane = jnp.arange(LANES)
    use_col = (lane % DIFF_DQK) >= (DIFF_DQK // 2)
    p = jnp.where(use_col[None, :], (pos % GRID_W)[:, None], (pos // GRID_W)[:, None]).astype(F32)
    ang = p * inv_freq[lane % quarter][None, :]
    sign = jnp.where((lane % (2 * quarter)) < quarter, -1.0, 1.0)[None, :]
    cos = jnp.concatenate([jnp.cos(ang), jnp.ones((c, LANES), F32)], axis=0)
    sin = jnp.concatenate([jnp.sin(ang) * sign, jnp.zeros((c, LANES), F32)], axis=0)
    return cos, sin


def kernel(x, c, ctx, c_ctx, w_ada, b_ada, w_in, conv_w, a_log, dt_bias, gdn_norm_w, lam_q1, lam_k1,
           lam_q2, lam_k2, diff_norm_w, w_proj_a, w_proj_b, w_out, ln1_g, ln1_b, w_gate, w_up, w_down,
           ln2_g, ln2_b):
    bsz, s, d = x.shape
    cl = ctx.shape[1]
    t = s + cl
    nh = d // HEAD_W
    qk_w = nh * GDN_DK
    v_w = nh * GDN_DV
    dqk_w = nh * 2 * DIFF_DQK
    dv_w = nh * DIFF_DV
    alpha = (2.0 * w_ada.shape[0]) ** 0.25
    assert w_ada.shape[0] == 1 and s % ROW_BLOCK == 0 and cl % ROW_BLOCK == 0 and 8 * nh <= LANES

    c8 = jnp.zeros((SUBLANES, d), F32).at[:bsz].set(c).at[bsz].set(c_ctx)
    mod = _ada(c8, w_ada[0], b_ada[0])

    h_all = _modulate(x, ctx, mod, bsz)

    w_l = jnp.swapaxes(w_in[0], 0, 1)
    z_col = 2 * qk_w + v_w
    ab_start = z_col + v_w
    d_start = ab_start + 4 * nh
    n_d = 2 * dqk_w + dv_w + 2 * d
    h2d = h_all.reshape(bsz * t, d)
    tm_in = _tile(bsz * t, 512, ROW_BLOCK)
    p_g = _in_proj(h2d, w_l, 0, ab_start, tm=tm_in, tn=_tile(ab_start, 1024, LANES),
                   name="in_proj_gdn").reshape(bsz, t, ab_start)
    p_ab = _in_proj(h2d, w_l, ab_start, LANES, tm=tm_in, tn=LANES,
                    name="in_proj_ab").reshape(bsz, t, LANES)
    p_d = _in_proj(h2d, w_l, d_start, n_d, tm=tm_in, tn=_tile(n_d, 1024, LANES),
                   name="in_proj_diff").reshape(bsz, t, n_d)
    ga_col = 2 * dqk_w + dv_w
    gb_col = ga_col + d

    w2 = 2 * nh
    alog128 = jnp.pad(a_log[0].reshape(1, w2), ((0, 0), (0, LANES - w2)))
    dtb128 = jnp.pad(dt_bias[0].reshape(1, w2), ((0, 0), (0, LANES - w2)))
    gb = _gates(p_ab, alog128, dtb128, w2)
    gr = jnp.swapaxes(gb[:, :, :w2], 1, 2)
    g_tot = gb[:, ::GDN_CHUNK, 3 * w2:4 * w2].reshape(-1)
    qkv = _gdn_conv(p_g, conv_w[0], nh, s)
    u, w_, qd, qk, kdt = _gdn_intra(qkv, gb, gr, nh)
    o_f, o_b = _gdn_scan(g_tot, u, w_, qd, qk, kdt, nh, s)
    y_a = _gdn_out(o_f, o_b, p_g, z_col // v_w, gdn_norm_w[0], nh, s)

    cos_t, sin_t = _rope_tables(s, cl)
    qr, kr, vr = _rope(p_d, cos_t, sin_t, nh)
    lam8 = jnp.zeros((SUBLANES, LANES), F32)
    for r, vec in enumerate((lam_q1, lam_k1, lam_q2, lam_k2)):
        lam8 = lam8.at[r, :DIFF_DQK].set(vec[0])
    y_b = _attention(qr, kr, vr, lam8, diff_norm_w[0], s)

    ymg = _merge(y_a, y_b, w_proj_a[0].astype(BF16), w_proj_b[0].astype(BF16), p_d, ga_col, gb_col)
    m = bsz * s
    x1, h2 = _proj_ln(ymg.reshape(m, d), w_out[0].astype(BF16), x.reshape(m, d), mod, 2,
                      ln1_g[0], ln1_b[0], s, alpha=alpha, tm=_tile(s, 512, ROW_BLOCK),
                      tn=_tile(d, 512, LANES), name="out_proj_ln", next_mod_chunks=(3, 4))

    hff = _ffn_up(h2, w_gate[0].astype(BF16), w_up[0].astype(BF16))
    out = _proj_ln(hff, w_down[0].astype(BF16), x1, mod, 5, ln2_g[0], ln2_b[0], s, alpha=alpha,
                   tm=_tile(s, 512, ROW_BLOCK), tn=_tile(d, 256, LANES), name="ffn_down_ln")
    return out.reshape(bsz, s, d)
```

```python
import functools
import math

import jax
import jax.numpy as jnp
from jax import lax
from jax.experimental import pallas as pl
from jax.experimental.pallas import tpu as pltpu

F32 = jnp.float32
BF16 = jnp.bfloat16

GRID_W = 64
HEAD_W = 256
GDN_DK = 128
GDN_DV = 128
GDN_CHUNK = 64
DIFF_DQK = 64
DIFF_DV = 128
ROPE_BASE = 10000.0
LN_EPS = 1e-6
LAM_INIT = 0.8 - 0.6 * math.exp(-0.3 * 0)
Q_SCALE = DIFF_DQK ** -0.5 * math.log2(math.e)

LANES = 128
SUBLANES = 8
PAIR = 2 * GDN_CHUNK
ROW_BLOCK = 256
VMEM_LIMIT = 56 * 1024 * 1024

_NT = (((1,), (1,)), ((), ()))


def _params(*sem):
    return pltpu.CompilerParams(dimension_semantics=sem, vmem_limit_bytes=VMEM_LIMIT)


def _tile(n, target, quantum):
    best = None
    t = quantum
    while t <= min(n, target):
        if n % t == 0:
            best = t
        t += quantum
    assert best is not None, (n, target, quantum)
    return best


def _silu(x):
    return x * jax.nn.sigmoid(x)


def _ada_kernel(c_ref, w_ref, b_ref, o_ref):
    c = _silu(c_ref[...])
    o_ref[...] = jnp.dot(c.astype(BF16), w_ref[...].astype(BF16),
                         preferred_element_type=F32) + b_ref[...]


def _ada(c8, w_ada, b_ada):
    d, n = w_ada.shape
    tn = _tile(n, 512, LANES)
    return pl.pallas_call(
        _ada_kernel,
        grid=(n // tn,),
        in_specs=[pl.BlockSpec((SUBLANES, d), lambda j: (0, 0)),
                  pl.BlockSpec((d, tn), lambda j: (0, j)),
                  pl.BlockSpec((1, tn), lambda j: (0, j))],
        out_specs=pl.BlockSpec((SUBLANES, tn), lambda j: (0, j)),
        out_shape=jax.ShapeDtypeStruct((SUBLANES, n), F32),
        compiler_params=_params("parallel"),
        name="ada",
    )(c8, w_ada, b_ada.reshape(1, n))


def _modulate_kernel(x_ref, ctx_ref, sh_ref, sc_ref, o_ref, *, n_lat, ctx_row):
    b = pl.program_id(0)
    i = pl.program_id(1)

    @pl.when(i < n_lat)
    def _():
        sc = sc_ref[pl.ds(b, 1), :]
        sh = sh_ref[pl.ds(b, 1), :]
        o_ref[0] = (x_ref[0] * (1.0 + sc) + sh).astype(BF16)

    @pl.when(i >= n_lat)
    def _():
        sc = sc_ref[ctx_row:ctx_row + 1, :]
        sh = sh_ref[ctx_row:ctx_row + 1, :]
        o_ref[0] = (ctx_ref[0] * (1.0 + sc) + sh).astype(BF16)


def _modulate(x, ctx, mod, ctx_row):
    bsz, s, d = x.shape
    c = ctx.shape[1]
    t = s + c
    rb = ROW_BLOCK
    n_lat = s // rb
    return pl.pallas_call(
        functools.partial(_modulate_kernel, n_lat=n_lat, ctx_row=ctx_row),
        grid=(bsz, t // rb),
        in_specs=[pl.BlockSpec((1, rb, d), lambda b, i: (b, jnp.minimum(i, n_lat - 1), 0)),
                  pl.BlockSpec((1, rb, d), lambda b, i: (b, jnp.maximum(i - n_lat, 0), 0)),
                  pl.BlockSpec((SUBLANES, d), lambda b, i: (0, 0)),
                  pl.BlockSpec((SUBLANES, d), lambda b, i: (0, 1))],
        out_specs=pl.BlockSpec((1, rb, d), lambda b, i: (b, i, 0)),
        out_shape=jax.ShapeDtypeStruct((bsz, t, d), BF16),
        compiler_params=_params("parallel", "arbitrary"),
        name="modulate",
    )(x, ctx, mod, mod)


def _in_proj_kernel(a_ref, w_hbm, o_ref, wb_ref, stage_ref, sem, *, row0, n_tiles, n_chunks):
    j = pl.program_id(0)
    i = pl.program_id(1)
    tn = wb_ref.shape[1]
    ch = tn // n_chunks

    def chunk_copy(tile, c, slot):
        rows = pl.ds(pl.multiple_of(row0 + tile * tn + c * ch, SUBLANES), ch)
        return pltpu.make_async_copy(w_hbm.at[rows, :], stage_ref.at[slot], sem.at[slot])

    @pl.when(jnp.logical_and(j == 0, i == 0))
    def _():
        chunk_copy(0, 0, 0).start()
        for c in range(n_chunks):
            if c + 1 < n_chunks:
                chunk_copy(0, c + 1, (c + 1) % 2).start()
            chunk_copy(0, c, c % 2).wait()
            wb_ref[0, c * ch:(c + 1) * ch, :] = stage_ref[c % 2].astype(BF16)

    has_next = j + 1 < n_tiles

    @pl.when(jnp.logical_and(has_next, i < n_chunks))
    def _():
        chunk_copy(j + 1, i, i % 2).start()

    @pl.when(jnp.logical_and(has_next, jnp.logical_and(i >= 1, i <= n_chunks)))
    def _():
        c = i - 1
        chunk_copy(j + 1, c, c % 2).wait()
        wb_ref[(j + 1) % 2, pl.ds(pl.multiple_of(c * ch, ch), ch), :] = stage_ref[c % 2].astype(BF16)

    o_ref[...] = lax.dot_general(a_ref[...], wb_ref[j % 2], _NT, preferred_element_type=F32)


def _in_proj(a, w_t, row0, n, *, tm, tn, name):
    m, k = a.shape
    n_tiles = n // tn
    n_row_tiles = m // tm
    assert n % tn == 0 and row0 % SUBLANES == 0 and n_row_tiles >= 2
    n_chunks = max(c for c in range(1, tn // LANES + 1)
                   if (tn // LANES) % c == 0 and c <= n_row_tiles - 1)
    return pl.pallas_call(
        functools.partial(_in_proj_kernel, row0=row0, n_tiles=n_tiles, n_chunks=n_chunks),
        grid=(n_tiles, n_row_tiles),
        in_specs=[pl.BlockSpec((tm, k), lambda j, i: (i, 0)),
                  pl.BlockSpec(memory_space=pl.ANY)],
        out_specs=pl.BlockSpec((tm, tn), lambda j, i: (i, j)),
        out_shape=jax.ShapeDtypeStruct((m, n), F32),
        scratch_shapes=[pltpu.VMEM((2, tn, k), BF16),
                        pltpu.VMEM((2, tn // n_chunks, k), F32),
                        pltpu.SemaphoreType.DMA((2,))],
        compiler_params=_params("arbitrary", "arbitrary"),
        name=name,
    )(a, w_t)


def _gates_kernel(p_ref, alog_ref, dtb_ref, o_ref, *, w2):
    p = p_ref[0]
    rb = p.shape[0]
    lane = lax.broadcasted_iota(jnp.int32, p.shape, 1)
    row = lax.broadcasted_iota(jnp.int32, p.shape, 0) % GDN_CHUNK
    xa = p + dtb_ref[...]
    softplus = jnp.maximum(xa, 0.0) + jnp.log(1.0 + jnp.exp(-jnp.abs(xa)))
    g = jnp.where(lane < w2, -jnp.exp(alog_ref[...]) * softplus, 0.0)
    pre = g
    suf = g
    step = 1
    while step < GDN_CHUNK:
        pre = pre + jnp.where(row >= step, pltpu.roll(pre, step, axis=0), 0.0)
        suf = suf + jnp.where(row < GDN_CHUNK - step, pltpu.roll(suf, rb - step, axis=0), 0.0)
        step *= 2
    tot = pre + suf - g
    cum = jnp.where(lane < w2 // 2, pre, suf)
    beta = jax.nn.sigmoid(p)
    out = jnp.where(lane < w2, cum,
                    jnp.where(lane < 2 * w2, beta,
                              jnp.where(lane < 3 * w2, pltpu.roll(tot, 2 * w2, axis=1),
                                        pltpu.roll(jnp.exp(tot), 3 * w2, axis=1))))
    o_ref[0] = jnp.where(lane < 4 * w2, out, 0.0)


def _gates(p_ab, alog128, dtb128, w2):
    bsz, t, _ = p_ab.shape
    rb = ROW_BLOCK
    return pl.pallas_call(
        functools.partial(_gates_kernel, w2=w2),
        grid=(bsz, t // rb),
        in_specs=[pl.BlockSpec((1, rb, LANES), lambda b, i: (b, i, 0)),
                  pl.BlockSpec((1, LANES), lambda b, i: (0, 0)),
                  pl.BlockSpec((1, LANES), lambda b, i: (0, 0))],
        out_specs=pl.BlockSpec((1, rb, LANES), lambda b, i: (b, i, 0)),
        out_shape=jax.ShapeDtypeStruct((bsz, t, LANES), F32),
        compiler_params=_params("parallel", "parallel"),
        name="gdn_gates",
    )(p_ab, alog128, dtb128)


def _conv_kernel(x_ref, cw_ref, o_ref, *, n_heads, cb, n_lat, n_blk, taps):
    j = pl.program_id(1)
    t = x_ref.shape[1]
    width = x_ref.shape[2]
    rb = ROW_BLOCK
    halo = SUBLANES
    pad = taps // 2
    win_rows = rb + 2 * halo
    w = cw_ref[...]
    is_qk = j * cb < 2 * n_heads
    q_scale = jnp.where(j * cb < n_heads, GDN_DK ** -0.5, 1.0)

    def body(r, carry):
        base = pl.multiple_of(r * rb, rb)
        top_ok = jnp.logical_and(r != 0, r != n_lat)
        bot_ok = jnp.logical_and(r != n_lat - 1, r != n_blk - 1)
        top = x_ref[0, pl.ds(pl.multiple_of(jnp.maximum(base - halo, 0), halo), halo), :]
        bot = x_ref[0, pl.ds(pl.multiple_of(jnp.minimum(base + rb, t - halo), halo), halo), :]
        win = jnp.concatenate([jnp.where(top_ok, top, 0.0), x_ref[0, pl.ds(base, rb), :],
                               jnp.where(bot_ok, bot, 0.0)], axis=0)
        acc = jnp.zeros((win_rows, width), F32)
        for tap in range(taps):
            shift = (pad - tap) % win_rows
            shifted = win if shift == 0 else pltpu.roll(win, shift, axis=0)
            acc = acc + shifted * w[tap:tap + 1, :]
        y = _silu(acc[halo:halo + rb, :])
        for c in range(cb):
            yc = y[:, c * LANES:(c + 1) * LANES]
            ss = jnp.sum(yc * yc, axis=-1, keepdims=True)
            scale = jnp.where(is_qk, lax.rsqrt(ss + 1e-6) * q_scale, 1.0)
            o_ref[0, c, pl.ds(base, rb), :] = yc * scale
        return carry

    lax.fori_loop(0, n_blk, body, 0)


def _gdn_conv(p_g, conv_w, n_heads, s):
    bsz, t, _ = p_g.shape
    taps = conv_w.shape[0]
    n_cols = 3 * n_heads
    cb = _tile(n_heads, 4, 1)
    return pl.pallas_call(
        functools.partial(_conv_kernel, n_heads=n_heads, cb=cb, n_lat=s // ROW_BLOCK,
                          n_blk=t // ROW_BLOCK, taps=taps),
        grid=(bsz, n_cols // cb),
        in_specs=[pl.BlockSpec((1, t, cb * LANES), lambda b, j: (b, 0, j)),
                  pl.BlockSpec((taps, cb * LANES), lambda b, j: (0, j))],
        out_specs=pl.BlockSpec((1, cb, t, LANES), lambda b, j: (b, j, 0, 0)),
        out_shape=jax.ShapeDtypeStruct((bsz, n_cols, t, LANES), F32),
        compiler_params=_params("parallel", "parallel"),
        name="gdn_conv",
    )(p_g, conv_w)


def _gdn_intra_kernel(q_ref, k_ref, v_ref, gb_ref, gr_ref,
                      u_ref, w_ref, qd_ref, qk_ref, kdt_ref, *, n_heads, hb):
    hblk = pl.program_id(1)
    w2 = 2 * n_heads
    rb = q_ref.shape[2]
    gbv = gb_ref[0]
    lane = lax.broadcasted_iota(jnp.int32, gbv.shape, 1)

    def column(idx):
        return jnp.sum(jnp.where(lane == idx, gbv, 0.0), axis=-1, keepdims=True)

    ri = lax.broadcasted_iota(jnp.int32, (PAIR, PAIR), 0)
    ci = lax.broadcasted_iota(jnp.int32, (PAIR, PAIR), 1)
    same = (ri >= GDN_CHUNK) == (ci >= GDN_CHUNK)
    eye_f32 = jnp.where(ri == ci, 1.0, 0.0)
    levels = int(math.log2(GDN_CHUNK))
    off_blocks = ([], [])
    for lvl in range(levels):
        half = 1 << lvl
        same_blk = (ri >> (lvl + 1)) == (ci >> (lvl + 1))
        r_hi = (ri & half) != 0
        c_hi = (ci & half) != 0
        off_blocks[0].append(same_blk & r_hi & jnp.logical_not(c_hi))
        off_blocks[1].append(same_blk & c_hi & jnp.logical_not(r_hi))

    incls = (jnp.logical_and(same, ri >= ci), jnp.logical_and(same, ri <= ci))
    stricts = (jnp.logical_and(same, ri > ci), jnp.logical_and(same, ri < ci))

    probs = []
    for hh in range(hb):
        head = hblk * hb + hh
        gcols = [column(d * n_heads + head) for d in range(2)]
        bcols = [column(w2 + d * n_heads + head) for d in range(2)]
        tcols = [column(2 * w2 + d * n_heads + head) for d in range(2)]
        grows = [gr_ref[0, pl.ds(d * n_heads + head, 1), :] for d in range(2)]
        for p in range(rb // PAIR):
            rows = slice(p * PAIR, (p + 1) * PAIR)
            q = q_ref[0, hh, rows, :]
            k = k_ref[0, hh, rows, :]
            kbf = k.astype(BF16)
            kk = lax.dot_general(kbf, kbf, _NT, preferred_element_type=F32)
            qkm = lax.dot_general(q.astype(BF16), kbf, _NT, preferred_element_type=F32)
            for d in range(2):
                gc = gcols[d][rows]
                bc = bcols[d][rows]
                dec = jnp.where(incls[d], jnp.exp(jnp.where(incls[d], gc - grows[d][:, rows], 0.0)), 0.0)
                a = jnp.where(stricts[d], kk * bc * dec, 0.0)
                qk_ref[0, d, hh, rows, :] = jnp.where(incls[d], qkm * dec, 0.0).astype(BF16)
                probs.append(dict(hh=hh, rows=rows, d=d, a=a, gc=gc, bc=bc, tc=tcols[d][rows]))

    minv = [eye_f32 - jnp.where(off_blocks[pr["d"]][0], pr["a"], 0.0) for pr in probs]
    a_bf = [pr["a"].astype(BF16) for pr in probs]
    for lvl in range(1, levels):
        mb = [m.astype(BF16) for m in minv]
        t1 = [jnp.dot(a_bf[i], mb[i], preferred_element_type=F32).astype(BF16)
              for i in range(len(probs))]
        minv = [minv[i] - jnp.where(off_blocks[pr["d"]][lvl],
                                    jnp.dot(mb[i], t1[i], preferred_element_type=F32), 0.0)
                for i, pr in enumerate(probs)]

    for i, pr in enumerate(probs):
        hh, rows, d, gc, bc = pr["hh"], pr["rows"], pr["d"], pr["gc"], pr["bc"]
        q = q_ref[0, hh, rows, :]
        k = k_ref[0, hh, rows, :]
        v = v_ref[0, hh, rows, :]
        eg = jnp.exp(gc)
        vb = v * bc
        kbg = k * (bc * eg)
        rhs = jnp.concatenate([vb, kbg], axis=1).astype(BF16)
        uw = jnp.dot((minv[i] - eye_f32).astype(BF16), rhs, preferred_element_type=F32)
        u_ref[0, d, hh, rows, :] = vb + uw[:, :GDN_DV]
        w_ref[0, d, hh, rows, :] = (kbg + uw[:, GDN_DV:]).astype(BF16)
        qd_ref[0, d, hh, rows, :] = (q * eg).astype(BF16)
        kdt_ref[0, d, hh, rows, :] = (k * jnp.exp(pr["tc"] - gc)).T.astype(BF16)


def _gdn_intra(qkv, gb, gr, n_heads):
    bsz, _, t, _ = qkv.shape
    rb = ROW_BLOCK
    nh = n_heads
    hb = _tile(nh, 4, 1)
    nhb = nh // hb

    def qkv_spec(part):
        return pl.BlockSpec((1, hb, rb, LANES), lambda b, h, i: (b, part * nhb + h, i, 0))

    out_spec = pl.BlockSpec((1, 2, hb, rb, LANES), lambda b, h, i: (b, 0, h, i, 0))
    shape = (bsz, 2, nh, t, LANES)
    return pl.pallas_call(
        functools.partial(_gdn_intra_kernel, n_heads=nh, hb=hb),
        grid=(bsz, nhb, t // rb),
        in_specs=[qkv_spec(0), qkv_spec(1), qkv_spec(2),
                  pl.BlockSpec((1, rb, LANES), lambda b, h, i: (b, i, 0)),
                  pl.BlockSpec((1, gr.shape[1], rb), lambda b, h, i: (b, 0, i))],
        out_specs=[out_spec] * 5,
        out_shape=[jax.ShapeDtypeStruct(shape, F32)] + [jax.ShapeDtypeStruct(shape, BF16)] * 4,
        compiler_params=_params("parallel", "parallel", "parallel"),
        name="gdn_intra",
    )(qkv, qkv, qkv, gb, gr)


def _gdn_scan_kernel(gt_ref, uf, wf, qdf, qkf, kdtf, ub, wb, qdb, qkb, kdtb,
                     of_ref, ob_ref, s_ref, *, n_heads, hb, n_lat_blk, n_blk):
    b = pl.program_id(0)
    hblk = pl.program_id(1)
    i = pl.program_id(2)
    rb = uf.shape[3]
    n_ch = rb // GDN_CHUNK
    w2 = 2 * n_heads

    @pl.when(i == 0)
    def _():
        s_ref[...] = jnp.zeros(s_ref.shape, F32)

    blk_f = (i + n_lat_blk) % n_blk
    blk_b = n_blk - 1 - i
    zeros = jnp.zeros((GDN_CHUNK, GDN_DV), BF16)
    dirs = ((0, uf, wf, qdf, qkf, kdtf, of_ref, blk_f),
            (1, ub, wb, qdb, qkb, kdtb, ob_ref, blk_b))
    for c in range(n_ch):
        chains = []
        for d, u_r, w_r, qd_r, qk_r, kdt_r, o_r, blk in dirs:
            cc = c if d == 0 else n_ch - 1 - c
            rows = slice(cc * GDN_CHUNK, (cc + 1) * GDN_CHUNK)
            pair_rows = slice((cc // 2) * PAIR, (cc // 2 + 1) * PAIR)
            for hh in range(hb):
                g_tot = gt_ref[(b * (n_blk * n_ch) + blk * n_ch + cc) * w2 + d * n_heads + hblk * hb + hh]
                chains.append((d, hh, cc, rows, pair_rows, u_r, w_r, qd_r, qk_r, kdt_r, o_r, g_tot))
        states = [s_ref[d, hh] for d, hh, *_ in chains]
        m1s = [jnp.dot(jnp.concatenate([w_r[0, 0, hh, rows, :], qd_r[0, 0, hh, rows, :]], axis=0),
                       states[n].astype(BF16), preferred_element_type=F32)
               for n, (d, hh, cc, rows, pair_rows, u_r, w_r, qd_r, qk_r, kdt_r, o_r, g_tot) in enumerate(chains)]
        m2s = []
        for n, (d, hh, cc, rows, pair_rows, u_r, w_r, qd_r, qk_r, kdt_r, o_r, g_tot) in enumerate(chains):
            v_new = (u_r[0, 0, hh, rows, :] - m1s[n][:GDN_CHUNK]).astype(BF16)
            v_ext = (jnp.concatenate([v_new, zeros], axis=0) if cc % 2 == 0
                     else jnp.concatenate([zeros, v_new], axis=0))
            lhs2 = jnp.concatenate([qk_r[0, 0, hh, rows, :], kdt_r[0, 0, hh, pair_rows, :]], axis=0)
            m2s.append(jnp.dot(lhs2, v_ext, preferred_element_type=F32))
        for n, (d, hh, cc, rows, pair_rows, u_r, w_r, qd_r, qk_r, kdt_r, o_r, g_tot) in enumerate(chains):
            s_ref[d, hh] = states[n] * g_tot + m2s[n][GDN_CHUNK:]
            o_r[0, rows, hh * GDN_DV:(hh + 1) * GDN_DV] = m1s[n][GDN_CHUNK:] + m2s[n][:GDN_CHUNK]


def _gdn_scan(g_tot, u, w, qd, qk, kdt, n_heads, s):
    bsz, _, nh, t, _ = u.shape
    rb = ROW_BLOCK
    hb = _tile(nh, 8, 1)
    n_blk = t // rb
    n_lat_blk = s // rb

    def spec(d):
        if d == 0:
            return pl.BlockSpec((1, 1, hb, rb, LANES),
                                lambda b, h, i: (b, 0, h, (i + n_lat_blk) % n_blk, 0))
        return pl.BlockSpec((1, 1, hb, rb, LANES), lambda b, h, i: (b, 1, h, n_blk - 1 - i, 0))

    o_shape = jax.ShapeDtypeStruct((bsz, t, nh * GDN_DV), F32)
    return pl.pallas_call(
        functools.partial(_gdn_scan_kernel, n_heads=nh, hb=hb, n_lat_blk=n_lat_blk, n_blk=n_blk),
        grid=(bsz, nh // hb, n_blk),
        in_specs=[pl.BlockSpec(memory_space=pltpu.SMEM)] + [spec(0)] * 5 + [spec(1)] * 5,
        out_specs=[pl.BlockSpec((1, rb, hb * GDN_DV), lambda b, h, i: (b, (i + n_lat_blk) % n_blk, h)),
                   pl.BlockSpec((1, rb, hb * GDN_DV), lambda b, h, i: (b, n_blk - 1 - i, h))],
        out_shape=[o_shape, o_shape],
        scratch_shapes=[pltpu.VMEM((2, hb, GDN_DK, GDN_DV), F32)],
        compiler_params=_params("parallel", "parallel", "arbitrary"),
        name="gdn_scan",
    )(g_tot, u, w, qd, qk, kdt, u, w, qd, qk, kdt)


def _gdn_out_kernel(of_ref, ob_ref, z_ref, nw_ref, o_ref, *, n_heads):
    nw = nw_ref[...]
    for h in range(n_heads):
        cols = slice(h * GDN_DV, (h + 1) * GDN_DV)
        o = of_ref[0, :, cols] + ob_ref[0, :, cols]
        ms = jnp.mean(o * o, axis=-1, keepdims=True)
        o_ref[0, :, cols] = (o * lax.rsqrt(ms + 1e-6) * nw * _silu(z_ref[0, :, cols])).astype(BF16)


def _gdn_out(o_f, o_b, p3, z_col_blk, gdn_norm_w, n_heads, s):
    bsz = o_f.shape[0]
    width = n_heads * GDN_DV
    tm = ROW_BLOCK
    return pl.pallas_call(
        functools.partial(_gdn_out_kernel, n_heads=n_heads),
        grid=(bsz, s // tm),
        in_specs=[pl.BlockSpec((1, tm, width), lambda b, i: (b, i, 0)),
                  pl.BlockSpec((1, tm, width), lambda b, i: (b, i, 0)),
                  pl.BlockSpec((1, tm, width), lambda b, i: (b, i, z_col_blk)),
                  pl.BlockSpec((1, GDN_DV), lambda b, i: (0, 0))],
        out_specs=pl.BlockSpec((1, tm, width), lambda b, i: (b, i, 0)),
        out_shape=jax.ShapeDtypeStruct((bsz, s, width), BF16),
        compiler_params=_params("parallel", "parallel"),
        name="gdn_out",
    )(o_f, o_b, p3, gdn_norm_w.reshape(1, GDN_DV))


def _rope_kernel(q_ref, k_ref, v_ref, cos_ref, sin_ref, qo_ref, ko_ref, vo_ref, *, n_heads):
    cos = cos_ref[...]
    sin = sin_ref[...]
    lane = lax.broadcasted_iota(jnp.int32, cos.shape, 1)
    first_half = (lane % (DIFF_DQK // 2)) < (DIFF_DQK // 4)
    quarter = DIFF_DQK // 4

    def rot(x):
        partner = jnp.where(first_half, pltpu.roll(x, LANES - quarter, axis=1),
                            pltpu.roll(x, quarter, axis=1))
        return x * cos + partner * sin

    for h in range(n_heads):
        cols = slice(h * LANES, (h + 1) * LANES)
        qo_ref[0, h] = (rot(q_ref[0, :, cols]) * Q_SCALE).astype(BF16)
        ko_ref[0, h] = rot(k_ref[0, :, cols]).astype(BF16)
        vo_ref[0, h] = v_ref[0, :, cols].astype(BF16)


def _rope(p_d, cos_t, sin_t, n_heads):
    bsz, t, _ = p_d.shape
    rb = ROW_BLOCK
    width = n_heads * LANES

    def in_spec(part):
        return pl.BlockSpec((1, rb, width), lambda b, i: (b, i, part))

    tab = pl.BlockSpec((rb, LANES), lambda b, i: (i, 0))
    out = pl.BlockSpec((1, n_heads, rb, LANES), lambda b, i: (b, 0, i, 0))
    shape = jax.ShapeDtypeStruct((bsz, n_heads, t, LANES), BF16)
    return pl.pallas_call(
        functools.partial(_rope_kernel, n_heads=n_heads),
        grid=(bsz, t // rb),
        in_specs=[in_spec(0), in_spec(1), in_spec(2), tab, tab],
        out_specs=[out, out, out],
        out_shape=[shape, shape, shape],
        compiler_params=_params("parallel", "parallel"),
        name="diff_rope",
    )(p_d, p_d, p_d, cos_t, sin_t)


def _attn_kernel(q_ref, k_ref, v_ref, lam_ref, nw_ref, o_ref, *, key_chunks):
    q = q_ref[0, 0]
    tq = q.shape[0]
    lane = lax.broadcasted_iota(jnp.int32, q.shape, 1)
    q2 = jnp.concatenate([jnp.where(lane < DIFF_DQK, q, jnp.zeros_like(q)),
                          jnp.where(lane >= DIFF_DQK, q, jnp.zeros_like(q))], axis=0)
    lp = lam_ref[...]
    lam = (jnp.exp(jnp.sum(lp[0:1] * lp[1:2], axis=-1, keepdims=True))
           - jnp.exp(jnp.sum(lp[2:3] * lp[3:4], axis=-1, keepdims=True)) + LAM_INIT)
    m = jnp.full((2 * tq, 1), -jnp.inf, F32)
    l = jnp.zeros((2 * tq, 1), F32)
    acc = jnp.zeros((2 * tq, DIFF_DV), F32)
    for start, size in key_chunks:
        k = k_ref[0, 0, start:start + size, :]
        v = v_ref[0, 0, start:start + size, :]
        s = lax.dot_general(q2, k, _NT, preferred_element_type=F32)
        m_new = jnp.maximum(m, jnp.max(s, axis=-1, keepdims=True))
        alpha = jnp.exp2(m - m_new)
        e = jnp.exp2(s - m_new)
        l = alpha * l + jnp.sum(e, axis=-1, keepdims=True)
        acc = alpha * acc + jnp.dot(e.astype(BF16), v, preferred_element_type=F32)
        m = m_new
    o = acc[:tq] / l[:tq] - lam * (acc[tq:] / l[tq:])
    ms = jnp.mean(o * o, axis=-1, keepdims=True)
    o_ref[0] = (o * lax.rsqrt(ms + 1e-5) * nw_ref[...] * (1.0 - LAM_INIT)).astype(BF16)


def _attention(qr, kr, vr, lam8, diff_norm_w, s):
    bsz, nh, t, _ = kr.shape
    tq = _tile(s, 512, ROW_BLOCK)
    kc = _tile(s, 2048, ROW_BLOCK)
    key_chunks = [(st, kc) for st in range(0, s, kc)] + [(s, t - s)]
    return pl.pallas_call(
        functools.partial(_attn_kernel, key_chunks=tuple(key_chunks)),
        grid=(bsz, nh, s // tq),
        in_specs=[pl.BlockSpec((1, 1, tq, LANES), lambda b, h, i: (b, h, i, 0)),
                  pl.BlockSpec((1, 1, t, LANES), lambda b, h, i: (b, h, 0, 0)),
                  pl.BlockSpec((1, 1, t, LANES), lambda b, h, i: (b, h, 0, 0)),
                  pl.BlockSpec((SUBLANES, LANES), lambda b, h, i: (0, 0)),
                  pl.BlockSpec((1, DIFF_DV), lambda b, h, i: (0, 0))],
        out_specs=pl.BlockSpec((1, tq, DIFF_DV), lambda b, h, i: (b, i, h)),
        out_shape=jax.ShapeDtypeStruct((bsz, s, nh * DIFF_DV), BF16),
        compiler_params=_params("parallel", "parallel", "arbitrary"),
        name="diff_attn",
    )(qr, kr, vr, lam8, diff_norm_w.reshape(1, DIFF_DV))


def _merge_kernel(ya_ref, yb_ref, wa_ref, wb_ref, ga_ref, gb_ref, o_ref):
    sa = jax.nn.sigmoid(ga_ref[0])
    sb = jax.nn.sigmoid(gb_ref[0])
    pa = jnp.dot(ya_ref[0], wa_ref[...], preferred_element_type=F32)
    pb = jnp.dot(yb_ref[0], wb_ref[...], preferred_element_type=F32)
    o_ref[0] = (sa * pa + sb * pb).astype(BF16)


def _merge(y_a, y_b, wa, wb, p3, ga_col, gb_col):
    bsz, s, ka = y_a.shape
    kb = y_b.shape[2]
    d = wa.shape[1]
    tm = _tile(s, 512, ROW_BLOCK)
    tn = _tile(math.gcd(d, ga_col, gb_col), 1024, LANES)
    return pl.pallas_call(
        _merge_kernel,
        grid=(bsz, s // tm, d // tn),
        in_specs=[pl.BlockSpec((1, tm, ka), lambda b, i, j: (b, i, 0)),
                  pl.BlockSpec((1, tm, kb), lambda b, i, j: (b, i, 0)),
                  pl.BlockSpec((ka, tn), lambda b, i, j: (0, j)),
                  pl.BlockSpec((kb, tn), lambda b, i, j: (0, j)),
                  pl.BlockSpec((1, tm, tn), lambda b, i, j: (b, i, ga_col // tn + j)),
                  pl.BlockSpec((1, tm, tn), lambda b, i, j: (b, i, gb_col // tn + j))],
        out_specs=pl.BlockSpec((1, tm, tn), lambda b, i, j: (b, i, j)),
        out_shape=jax.ShapeDtypeStruct((bsz, s, d), BF16),
        compiler_params=_params("parallel", "parallel", "arbitrary"),
        name="merge",
    )(y_a, y_b, wa, wb, p3, p3)


def _layernorm_rows(ref, g, bvec, emit, rows_per_step=32):
    tm = ref.shape[0]

    def body(r, carry):
        rows = pl.ds(pl.multiple_of(r * rows_per_step, rows_per_step), rows_per_step)
        x = ref[rows, :]
        mu = jnp.mean(x, axis=-1, keepdims=True)
        xc = x - mu
        var = jnp.mean(xc * xc, axis=-1, keepdims=True)
        y = xc * lax.rsqrt(var + LN_EPS) * g + bvec
        ref[rows, :] = y
        emit(rows, y)
        return carry

    lax.fori_loop(0, tm // rows_per_step, body, 0)


def _proj_ln_kernel(a_ref, w_ref, x_ref, gt_ref, g_ref, b_ref, sc_ref, sh_ref,
                    o_ref, h_ref, *, alpha, n_j, tn, rows_per_batch):
    i = pl.program_id(0)
    j = pl.program_id(1)
    tm = a_ref.shape[0]
    b = (i * tm) // rows_per_batch
    mix = jnp.dot(a_ref[...], w_ref[...], preferred_element_type=F32)
    r = alpha * x_ref[...] + gt_ref[pl.ds(b, 1), :] * mix
    for jj in range(n_j):
        @pl.when(j == jj)
        def _(jj=jj):
            o_ref[:, jj * tn:(jj + 1) * tn] = r

    @pl.when(j == n_j - 1)
    def _():
        if h_ref is None:
            _layernorm_rows(o_ref, g_ref[...], b_ref[...], lambda rows, y: None)
        else:
            sc = 1.0 + sc_ref[pl.ds(b, 1), :]
            sh = sh_ref[pl.ds(b, 1), :]

            def emit(rows, y):
                h_ref[rows, :] = (y * sc + sh).astype(BF16)

            _layernorm_rows(o_ref, g_ref[...], b_ref[...], emit)


def _proj_ln_kernel_noh(a_ref, w_ref, x_ref, gt_ref, g_ref, b_ref, o_ref, **kw):
    _proj_ln_kernel(a_ref, w_ref, x_ref, gt_ref, g_ref, b_ref, None, None, o_ref, None, **kw)


def _proj_ln(a, w, x, mod, gt_chunk, ln_g, ln_b, rows_per_batch, *, alpha, tm, tn, name,
             next_mod_chunks=None):
    m, k = a.shape
    d = w.shape[1]
    n_j = d // tn
    kw = dict(alpha=alpha, n_j=n_j, tn=tn, rows_per_batch=rows_per_batch)
    in_specs = [pl.BlockSpec((tm, k), lambda i, j: (i, 0)),
                pl.BlockSpec((k, tn), lambda i, j: (0, j)),
                pl.BlockSpec((tm, tn), lambda i, j: (i, j)),
                pl.BlockSpec((SUBLANES, tn), lambda i, j: (0, gt_chunk * n_j + j)),
                pl.BlockSpec((1, d), lambda i, j: (0, 0)),
                pl.BlockSpec((1, d), lambda i, j: (0, 0))]
    args = [a, w, x, mod, ln_g.reshape(1, d), ln_b.reshape(1, d)]
    row_spec = pl.BlockSpec((tm, d), lambda i, j: (i, 0))
    if next_mod_chunks is None:
        kernel = functools.partial(_proj_ln_kernel_noh, **kw)
        out_specs = row_spec
        out_shape = jax.ShapeDtypeStruct((m, d), F32)
    else:
        sh_chunk, sc_chunk = next_mod_chunks
        kernel = functools.partial(_proj_ln_kernel, **kw)
        in_specs += [pl.BlockSpec((SUBLANES, d), lambda i, j: (0, sc_chunk)),
                     pl.BlockSpec((SUBLANES, d), lambda i, j: (0, sh_chunk))]
        args += [mod, mod]
        out_specs = [row_spec, row_spec]
        out_shape = [jax.ShapeDtypeStruct((m, d), F32), jax.ShapeDtypeStruct((m, d), BF16)]
    return pl.pallas_call(
        kernel,
        grid=(m // tm, n_j),
        in_specs=in_specs,
        out_specs=out_specs,
        out_shape=out_shape,
        compiler_params=_params("parallel", "arbitrary"),
        name=name,
    )(*args)


def _ffn_up_kernel(h_ref, wg_ref, wu_ref, o_ref):
    h = h_ref[...]
    g = jnp.dot(h, wg_ref[...].astype(BF16), preferred_element_type=F32)
    u = jnp.dot(h, wu_ref[...].astype(BF16), preferred_element_type=F32)
    o_ref[...] = (_silu(g) * u).astype(BF16)


def _ffn_up(h, wg, wu):
    m, d = h.shape
    f = wg.shape[1]
    tm = _tile(m, 2048, ROW_BLOCK)
    tn = _tile(f, 512, LANES)
    return pl.pallas_call(
        _ffn_up_kernel,
        grid=(m // tm, f // tn),
        in_specs=[pl.BlockSpec((tm, d), lambda i, j: (i, 0)),
                  pl.BlockSpec((d, tn), lambda i, j: (0, j)),
                  pl.BlockSpec((d, tn), lambda i, j: (0, j))],
        out_specs=pl.BlockSpec((tm, tn), lambda i, j: (i, j)),
        out_shape=jax.ShapeDtypeStruct((m, f), BF16),
        compiler_params=_params("parallel", "arbitrary"),
        name="ffn_up",
    )(h, wg, wu)


def _rope_tables(s, c):
    quarter = DIFF_DQK // 4
    inv_freq = ROPE_BASE ** (-jnp.arange(quarter, dtype=F32) / quarter)
    pos = jnp.arange(s, dtype=jnp.int32)
    lane = jnp.arange(LANES)
    use_col = (lane % DIFF_DQK) >= (DIFF_DQK // 2)
    p = jnp.where(use_col[None, :], (pos % GRID_W)[:, None], (pos // GRID_W)[:, None]).astype(F32)
    ang = p * inv_freq[lane % quarter][None, :]
    sign = jnp.where((lane % (2 * quarter)) < quarter, -1.0, 1.0)[None, :]
    cos = jnp.concatenate([jnp.cos(ang), jnp.ones((c, LANES), F32)], axis=0)
    sin = jnp.concatenate([jnp.sin(ang) * sign, jnp.zeros((c, LANES), F32)], axis=0)
    return cos, sin


def kernel(x, c, ctx, c_ctx, w_ada, b_ada, w_in, conv_w, a_log, dt_bias, gdn_norm_w, lam_q1, lam_k1,
           lam_q2, lam_k2, diff_norm_w, w_proj_a, w_proj_b, w_out, ln1_g, ln1_b, w_gate, w_up, w_down,
           ln2_g, ln2_b):
    bsz, s, d = x.shape
    cl = ctx.shape[1]
    t = s + cl
    nh = d // HEAD_W
    qk_w = nh * GDN_DK
    v_w = nh * GDN_DV
    dqk_w = nh * 2 * DIFF_DQK
    dv_w = nh * DIFF_DV
    alpha = (2.0 * w_ada.shape[0]) ** 0.25
    assert w_ada.shape[0] == 1 and s % ROW_BLOCK == 0 and cl % ROW_BLOCK == 0 and 8 * nh <= LANES

    c8 = jnp.zeros((SUBLANES, d), F32).at[:bsz].set(c).at[bsz].set(c_ctx)
    mod = _ada(c8, w_ada[0], b_ada[0])

    h_all = _modulate(x, ctx, mod, bsz)

    w_l = jnp.swapaxes(w_in[0], 0, 1)
    z_col = 2 * qk_w + v_w
    ab_start = z_col + v_w
    d_start = ab_start + 4 * nh
    n_d = 2 * dqk_w + dv_w + 2 * d
    h2d = h_all.reshape(bsz * t, d)
    tm_in = _tile(bsz * t, 512, ROW_BLOCK)
    p_g = _in_proj(h2d, w_l, 0, ab_start, tm=tm_in, tn=_tile(ab_start, 1024, LANES),
                   name="in_proj_gdn").reshape(bsz, t, ab_start)
    p_ab = _in_proj(h2d, w_l, ab_start, LANES, tm=tm_in, tn=LANES,
                    name="in_proj_ab").reshape(bsz, t, LANES)
    p_d = _in_proj(h2d, w_l, d_start, n_d, tm=tm_in, tn=_tile(n_d, 1024, LANES),
                   name="in_proj_diff").reshape(bsz, t, n_d)
    ga_col = 2 * dqk_w + dv_w
    gb_col = ga_col + d

    w2 = 2 * nh
    alog128 = jnp.pad(a_log[0].reshape(1, w2), ((0, 0), (0, LANES - w2)))
    dtb128 = jnp.pad(dt_bias[0].reshape(1, w2), ((0, 0), (0, LANES - w2)))
    gb = _gates(p_ab, alog128, dtb128, w2)
    gr = jnp.swapaxes(gb[:, :, :w2], 1, 2)
    g_tot = gb[:, ::GDN_CHUNK, 3 * w2:4 * w2].reshape(-1)
    qkv = _gdn_conv(p_g, conv_w[0], nh, s)
    u, w_, qd, qk, kdt = _gdn_intra(qkv, gb, gr, nh)
    o_f, o_b = _gdn_scan(g_tot, u, w_, qd, qk, kdt, nh, s)
    y_a = _gdn_out(o_f, o_b, p_g, z_col // v_w, gdn_norm_w[0], nh, s)

    cos_t, sin_t = _rope_tables(s, cl)
    qr, kr, vr = _rope(p_d, cos_t, sin_t, nh)
    lam8 = jnp.zeros((SUBLANES, LANES), F32)
    for r, vec in enumerate((lam_q1, lam_k1, lam_q2, lam_k2)):
        lam8 = lam8.at[r, :DIFF_DQK].set(vec[0])
    y_b = _attention(qr, kr, vr, lam8, diff_norm_w[0], s)

    ymg = _merge(y_a, y_b, w_proj_a[0].astype(BF16), w_proj_b[0].astype(BF16), p_d, ga_col, gb_col)
    m = bsz * s
    x1, h2 = _proj_ln(ymg.reshape(m, d), w_out[0].astype(BF16), x.reshape(m, d), mod, 2,
                      ln1_g[0], ln1_b[0], s, alpha=alpha, tm=_tile(s, 512, ROW_BLOCK),
                      tn=_tile(d, 512, LANES), name="out_proj_ln", next_mod_chunks=(3, 4))

    hff = _ffn_up(h2, w_gate[0], w_up[0])
    out = _proj_ln(hff, w_down[0].astype(BF16), x1, mod, 5, ln2_g[0], ln2_b[0], s, alpha=alpha,
                   tm=_tile(s, 512, ROW_BLOCK), tn=_tile(d, 256, LANES), name="ffn_down_ln")
    return out.reshape(bsz, s, d)
```

```python
import functools
import math

import jax
import jax.numpy as jnp
from jax import lax
from jax.experimental import pallas as pl
from jax.experimental.pallas import tpu as pltpu

F32 = jnp.float32
BF16 = jnp.bfloat16

GRID_W = 64
HEAD_W = 256
GDN_DK = 128
GDN_DV = 128
GDN_CHUNK = 64
DIFF_DQK = 64
DIFF_DV = 128
ROPE_BASE = 10000.0
LN_EPS = 1e-6
LAM_INIT = 0.8 - 0.6 * math.exp(-0.3 * 0)
Q_SCALE = DIFF_DQK ** -0.5 * math.log2(math.e)

LANES = 128
SUBLANES = 8
PAIR = 2 * GDN_CHUNK
ROW_BLOCK = 256
VMEM_LIMIT = 56 * 1024 * 1024

_NT = (((1,), (1,)), ((), ()))


def _params(*sem):
    return pltpu.CompilerParams(dimension_semantics=sem, vmem_limit_bytes=VMEM_LIMIT)


def _tile(n, target, quantum):
    best = None
    t = quantum
    while t <= min(n, target):
        if n % t == 0:
            best = t
        t += quantum
    assert best is not None, (n, target, quantum)
    return best


def _silu(x):
    return x * jax.nn.sigmoid(x)


def _ada_kernel(c_ref, w_ref, b_ref, o_ref):
    c = _silu(c_ref[...])
    o_ref[...] = jnp.dot(c.astype(BF16), w_ref[...].astype(BF16),
                         preferred_element_type=F32) + b_ref[...]


def _ada(c8, w_ada, b_ada):
    d, n = w_ada.shape
    tn = _tile(n, 512, LANES)
    return pl.pallas_call(
        _ada_kernel,
        grid=(n // tn,),
        in_specs=[pl.BlockSpec((SUBLANES, d), lambda j: (0, 0)),
                  pl.BlockSpec((d, tn), lambda j: (0, j)),
                  pl.BlockSpec((1, tn), lambda j: (0, j))],
        out_specs=pl.BlockSpec((SUBLANES, tn), lambda j: (0, j)),
        out_shape=jax.ShapeDtypeStruct((SUBLANES, n), F32),
        compiler_params=_params("parallel"),
        name="ada",
    )(c8, w_ada, b_ada.reshape(1, n))


def _modulate_kernel(x_ref, ctx_ref, sh_ref, sc_ref, o_ref, *, n_lat, ctx_row):
    b = pl.program_id(0)
    i = pl.program_id(1)

    @pl.when(i < n_lat)
    def _():
        sc = sc_ref[pl.ds(b, 1), :]
        sh = sh_ref[pl.ds(b, 1), :]
        o_ref[0] = (x_ref[0] * (1.0 + sc) + sh).astype(BF16)

    @pl.when(i >= n_lat)
    def _():
        sc = sc_ref[ctx_row:ctx_row + 1, :]
        sh = sh_ref[ctx_row:ctx_row + 1, :]
        o_ref[0] = (ctx_ref[0] * (1.0 + sc) + sh).astype(BF16)


def _modulate(x, ctx, mod, ctx_row):
    bsz, s, d = x.shape
    c = ctx.shape[1]
    t = s + c
    rb = ROW_BLOCK
    n_lat = s // rb
    return pl.pallas_call(
        functools.partial(_modulate_kernel, n_lat=n_lat, ctx_row=ctx_row),
        grid=(bsz, t // rb),
        in_specs=[pl.BlockSpec((1, rb, d), lambda b, i: (b, jnp.minimum(i, n_lat - 1), 0)),
                  pl.BlockSpec((1, rb, d), lambda b, i: (b, jnp.maximum(i - n_lat, 0), 0)),
                  pl.BlockSpec((SUBLANES, d), lambda b, i: (0, 0)),
                  pl.BlockSpec((SUBLANES, d), lambda b, i: (0, 1))],
        out_specs=pl.BlockSpec((1, rb, d), lambda b, i: (b, i, 0)),
        out_shape=jax.ShapeDtypeStruct((bsz, t, d), BF16),
        compiler_params=_params("parallel", "arbitrary"),
        name="modulate",
    )(x, ctx, mod, mod)


def _in_proj_kernel(a_ref, w_hbm, o_ref, wb_ref, stage_ref, sem, *, row0, n_tiles, n_chunks):
    j = pl.program_id(0)
    i = pl.program_id(1)
    tn = wb_ref.shape[1]
    ch = tn // n_chunks

    def chunk_copy(tile, c, slot):
        rows = pl.ds(pl.multiple_of(row0 + tile * tn + c * ch, SUBLANES), ch)
        return pltpu.make_async_copy(w_hbm.at[rows, :], stage_ref.at[slot], sem.at[slot])

    @pl.when(jnp.logical_and(j == 0, i == 0))
    def _():
        chunk_copy(0, 0, 0).start()
        for c in range(n_chunks):
            if c + 1 < n_chunks:
                chunk_copy(0, c + 1, (c + 1) % 2).start()
            chunk_copy(0, c, c % 2).wait()
            wb_ref[0, c * ch:(c + 1) * ch, :] = stage_ref[c % 2].astype(BF16)

    has_next = j + 1 < n_tiles

    @pl.when(jnp.logical_and(has_next, i < n_chunks))
    def _():
        chunk_copy(j + 1, i, i % 2).start()

    @pl.when(jnp.logical_and(has_next, jnp.logical_and(i >= 1, i <= n_chunks)))
    def _():
        c = i - 1
        chunk_copy(j + 1, c, c % 2).wait()
        wb_ref[(j + 1) % 2, pl.ds(pl.multiple_of(c * ch, ch), ch), :] = stage_ref[c % 2].astype(BF16)

    o_ref[...] = lax.dot_general(a_ref[...], wb_ref[j % 2], _NT, preferred_element_type=F32)


def _in_proj(a, w_t, row0, n, *, tm, tn, name):
    m, k = a.shape
    n_tiles = n // tn
    n_row_tiles = m // tm
    assert n % tn == 0 and row0 % SUBLANES == 0 and n_row_tiles >= 2
    n_chunks = max(c for c in range(1, tn // LANES + 1)
                   if (tn // LANES) % c == 0 and c <= n_row_tiles - 1)
    return pl.pallas_call(
        functools.partial(_in_proj_kernel, row0=row0, n_tiles=n_tiles, n_chunks=n_chunks),
        grid=(n_tiles, n_row_tiles),
        in_specs=[pl.BlockSpec((tm, k), lambda j, i: (i, 0)),
                  pl.BlockSpec(memory_space=pl.ANY)],
        out_specs=pl.BlockSpec((tm, tn), lambda j, i: (i, j)),
        out_shape=jax.ShapeDtypeStruct((m, n), F32),
        scratch_shapes=[pltpu.VMEM((2, tn, k), BF16),
                        pltpu.VMEM((2, tn // n_chunks, k), F32),
                        pltpu.SemaphoreType.DMA((2,))],
        compiler_params=_params("arbitrary", "arbitrary"),
        name=name,
    )(a, w_t)


def _gates_kernel(p_ref, alog_ref, dtb_ref, o_ref, *, w2):
    p = p_ref[0]
    rb = p.shape[0]
    lane = lax.broadcasted_iota(jnp.int32, p.shape, 1)
    row = lax.broadcasted_iota(jnp.int32, p.shape, 0) % GDN_CHUNK
    xa = p + dtb_ref[...]
    softplus = jnp.maximum(xa, 0.0) + jnp.log(1.0 + jnp.exp(-jnp.abs(xa)))
    g = jnp.where(lane < w2, -jnp.exp(alog_ref[...]) * softplus, 0.0)
    pre = g
    suf = g
    step = 1
    while step < GDN_CHUNK:
        pre = pre + jnp.where(row >= step, pltpu.roll(pre, step, axis=0), 0.0)
        suf = suf + jnp.where(row < GDN_CHUNK - step, pltpu.roll(suf, rb - step, axis=0), 0.0)
        step *= 2
    tot = pre + suf - g
    cum = jnp.where(lane < w2 // 2, pre, suf)
    beta = jax.nn.sigmoid(p)
    out = jnp.where(lane < w2, cum,
                    jnp.where(lane < 2 * w2, beta,
                              jnp.where(lane < 3 * w2, pltpu.roll(tot, 2 * w2, axis=1),
                                        pltpu.roll(jnp.exp(tot), 3 * w2, axis=1))))
    o_ref[0] = jnp.where(lane < 4 * w2, out, 0.0)


def _gates(p_ab, alog128, dtb128, w2):
    bsz, t, _ = p_ab.shape
    rb = ROW_BLOCK
    return pl.pallas_call(
        functools.partial(_gates_kernel, w2=w2),
        grid=(bsz, t // rb),
        in_specs=[pl.BlockSpec((1, rb, LANES), lambda b, i: (b, i, 0)),
                  pl.BlockSpec((1, LANES), lambda b, i: (0, 0)),
                  pl.BlockSpec((1, LANES), lambda b, i: (0, 0))],
        out_specs=pl.BlockSpec((1, rb, LANES), lambda b, i: (b, i, 0)),
        out_shape=jax.ShapeDtypeStruct((bsz, t, LANES), F32),
        compiler_params=_params("parallel", "parallel"),
        name="gdn_gates",
    )(p_ab, alog128, dtb128)


def _conv_kernel(x_ref, cw_ref, o_ref, *, n_heads, cb, n_lat, n_blk, taps):
    j = pl.program_id(1)
    t = x_ref.shape[1]
    width = x_ref.shape[2]
    rb = ROW_BLOCK
    halo = SUBLANES
    pad = taps // 2
    win_rows = rb + 2 * halo
    w = cw_ref[...]
    is_qk = j * cb < 2 * n_heads
    q_scale = jnp.where(j * cb < n_heads, GDN_DK ** -0.5, 1.0)

    def body(r, carry):
        base = pl.multiple_of(r * rb, rb)
        top_ok = jnp.logical_and(r != 0, r != n_lat)
        bot_ok = jnp.logical_and(r != n_lat - 1, r != n_blk - 1)
        top = x_ref[0, pl.ds(pl.multiple_of(jnp.maximum(base - halo, 0), halo), halo), :]
        bot = x_ref[0, pl.ds(pl.multiple_of(jnp.minimum(base + rb, t - halo), halo), halo), :]
        win = jnp.concatenate([jnp.where(top_ok, top, 0.0), x_ref[0, pl.ds(base, rb), :],
                               jnp.where(bot_ok, bot, 0.0)], axis=0)
        acc = jnp.zeros((win_rows, width), F32)
        for tap in range(taps):
            shift = (pad - tap) % win_rows
            shifted = win if shift == 0 else pltpu.roll(win, shift, axis=0)
            acc = acc + shifted * w[tap:tap + 1, :]
        y = _silu(acc[halo:halo + rb, :])
        for c in range(cb):
            yc = y[:, c * LANES:(c + 1) * LANES]
            ss = jnp.sum(yc * yc, axis=-1, keepdims=True)
            scale = jnp.where(is_qk, lax.rsqrt(ss + 1e-6) * q_scale, 1.0)
            o_ref[0, c, pl.ds(base, rb), :] = yc * scale
        return carry

    lax.fori_loop(0, n_blk, body, 0)


def _gdn_conv(p_g, conv_w, n_heads, s):
    bsz, t, _ = p_g.shape
    taps = conv_w.shape[0]
    n_cols = 3 * n_heads
    cb = _tile(n_heads, 4, 1)
    return pl.pallas_call(
        functools.partial(_conv_kernel, n_heads=n_heads, cb=cb, n_lat=s // ROW_BLOCK,
                          n_blk=t // ROW_BLOCK, taps=taps),
        grid=(bsz, n_cols // cb),
        in_specs=[pl.BlockSpec((1, t, cb * LANES), lambda b, j: (b, 0, j)),
                  pl.BlockSpec((taps, cb * LANES), lambda b, j: (0, j))],
        out_specs=pl.BlockSpec((1, cb, t, LANES), lambda b, j: (b, j, 0, 0)),
        out_shape=jax.ShapeDtypeStruct((bsz, n_cols, t, LANES), F32),
        compiler_params=_params("parallel", "parallel"),
        name="gdn_conv",
    )(p_g, conv_w)


def _gdn_intra_kernel(q_ref, k_ref, v_ref, gb_ref, gr_ref,
                      u_ref, w_ref, qd_ref, qk_ref, kdt_ref, *, n_heads, hb):
    hblk = pl.program_id(1)
    w2 = 2 * n_heads
    rb = q_ref.shape[2]
    gbv = gb_ref[0]
    lane = lax.broadcasted_iota(jnp.int32, gbv.shape, 1)

    def column(idx):
        return jnp.sum(jnp.where(lane == idx, gbv, 0.0), axis=-1, keepdims=True)

    ri = lax.broadcasted_iota(jnp.int32, (PAIR, PAIR), 0)
    ci = lax.broadcasted_iota(jnp.int32, (PAIR, PAIR), 1)
    same = (ri >= GDN_CHUNK) == (ci >= GDN_CHUNK)
    eye_f32 = jnp.where(ri == ci, 1.0, 0.0)
    levels = int(math.log2(GDN_CHUNK))
    off_blocks = ([], [])
    for lvl in range(levels):
        half = 1 << lvl
        same_blk = (ri >> (lvl + 1)) == (ci >> (lvl + 1))
        r_hi = (ri & half) != 0
        c_hi = (ci & half) != 0
        off_blocks[0].append(same_blk & r_hi & jnp.logical_not(c_hi))
        off_blocks[1].append(same_blk & c_hi & jnp.logical_not(r_hi))

    incls = (jnp.logical_and(same, ri >= ci), jnp.logical_and(same, ri <= ci))
    stricts = (jnp.logical_and(same, ri > ci), jnp.logical_and(same, ri < ci))

    probs = []
    for hh in range(hb):
        head = hblk * hb + hh
        gcols = [column(d * n_heads + head) for d in range(2)]
        bcols = [column(w2 + d * n_heads + head) for d in range(2)]
        tcols = [column(2 * w2 + d * n_heads + head) for d in range(2)]
        grows = [gr_ref[0, pl.ds(d * n_heads + head, 1), :] for d in range(2)]
        for p in range(rb // PAIR):
            rows = slice(p * PAIR, (p + 1) * PAIR)
            q = q_ref[0, hh, rows, :]
            k = k_ref[0, hh, rows, :]
            kbf = k.astype(BF16)
            kk = lax.dot_general(kbf, kbf, _NT, preferred_element_type=F32)
            qkm = lax.dot_general(q.astype(BF16), kbf, _NT, preferred_element_type=F32)
            for d in range(2):
                gc = gcols[d][rows]
                bc = bcols[d][rows]
                dec = jnp.where(incls[d], jnp.exp(jnp.where(incls[d], gc - grows[d][:, rows], 0.0)), 0.0)
                a = jnp.where(stricts[d], kk * bc * dec, 0.0)
                qk_ref[0, d, hh, rows, :] = jnp.where(incls[d], qkm * dec, 0.0).astype(BF16)
                probs.append(dict(hh=hh, rows=rows, d=d, a=a, gc=gc, bc=bc, tc=tcols[d][rows]))

    minv = [eye_f32 - jnp.where(off_blocks[pr["d"]][0], pr["a"], 0.0) for pr in probs]
    a_bf = [pr["a"].astype(BF16) for pr in probs]
    for lvl in range(1, levels):
        mb = [m.astype(BF16) for m in minv]
        t1 = [jnp.dot(a_bf[i], mb[i], preferred_element_type=F32).astype(BF16)
              for i in range(len(probs))]
        minv = [minv[i] - jnp.where(off_blocks[pr["d"]][lvl],
                                    jnp.dot(mb[i], t1[i], preferred_element_type=F32), 0.0)
                for i, pr in enumerate(probs)]

    for i, pr in enumerate(probs):
        hh, rows, d, gc, bc = pr["hh"], pr["rows"], pr["d"], pr["gc"], pr["bc"]
        q = q_ref[0, hh, rows, :]
        k = k_ref[0, hh, rows, :]
        v = v_ref[0, hh, rows, :]
        eg = jnp.exp(gc)
        vb = v * bc
        kbg = k * (bc * eg)
        rhs = jnp.concatenate([vb, kbg], axis=1).astype(BF16)
        uw = jnp.dot((minv[i] - eye_f32).astype(BF16), rhs, preferred_element_type=F32)
        u_ref[0, d, hh, rows, :] = vb + uw[:, :GDN_DV]
        w_ref[0, d, hh, rows, :] = (kbg + uw[:, GDN_DV:]).astype(BF16)
        qd_ref[0, d, hh, rows, :] = (q * eg).astype(BF16)
        kdt_ref[0, d, hh, rows, :] = (k * jnp.exp(pr["tc"] - gc)).T.astype(BF16)


def _gdn_intra(qkv, gb, gr, n_heads):
    bsz, _, t, _ = qkv.shape
    rb = ROW_BLOCK
    nh = n_heads
    hb = _tile(nh, 4, 1)
    nhb = nh // hb

    def qkv_spec(part):
        return pl.BlockSpec((1, hb, rb, LANES), lambda b, h, i: (b, part * nhb + h, i, 0))

    out_spec = pl.BlockSpec((1, 2, hb, rb, LANES), lambda b, h, i: (b, 0, h, i, 0))
    shape = (bsz, 2, nh, t, LANES)
    return pl.pallas_call(
        functools.partial(_gdn_intra_kernel, n_heads=nh, hb=hb),
        grid=(bsz, nhb, t // rb),
        in_specs=[qkv_spec(0), qkv_spec(1), qkv_spec(2),
                  pl.BlockSpec((1, rb, LANES), lambda b, h, i: (b, i, 0)),
                  pl.BlockSpec((1, gr.shape[1], rb), lambda b, h, i: (b, 0, i))],
        out_specs=[out_spec] * 5,
        out_shape=[jax.ShapeDtypeStruct(shape, F32)] + [jax.ShapeDtypeStruct(shape, BF16)] * 4,
        compiler_params=_params("parallel", "parallel", "parallel"),
        name="gdn_intra",
    )(qkv, qkv, qkv, gb, gr)


def _gdn_scan_kernel(gt_ref, uf, wf, qdf, qkf, kdtf, ub, wb, qdb, qkb, kdtb,
                     of_ref, ob_ref, s_ref, *, n_heads, hb, n_lat_blk, n_blk):
    b = pl.program_id(0)
    hblk = pl.program_id(1)
    i = pl.program_id(2)
    rb = uf.shape[3]
    n_ch = rb // GDN_CHUNK
    w2 = 2 * n_heads

    @pl.when(i == 0)
    def _():
        s_ref[...] = jnp.zeros(s_ref.shape, F32)

    blk_f = (i + n_lat_blk) % n_blk
    blk_b = n_blk - 1 - i
    zeros = jnp.zeros((GDN_CHUNK, GDN_DV), BF16)
    dirs = ((0, uf, wf, qdf, qkf, kdtf, of_ref, blk_f),
            (1, ub, wb, qdb, qkb, kdtb, ob_ref, blk_b))
    for c in range(n_ch):
        chains = []
        for d, u_r, w_r, qd_r, qk_r, kdt_r, o_r, blk in dirs:
            cc = c if d == 0 else n_ch - 1 - c
            rows = slice(cc * GDN_CHUNK, (cc + 1) * GDN_CHUNK)
            pair_rows = slice((cc // 2) * PAIR, (cc // 2 + 1) * PAIR)
            for hh in range(hb):
                g_tot = gt_ref[(b * (n_blk * n_ch) + blk * n_ch + cc) * w2 + d * n_heads + hblk * hb + hh]
                chains.append((d, hh, cc, rows, pair_rows, u_r, w_r, qd_r, qk_r, kdt_r, o_r, g_tot))
        states = [s_ref[d, hh] for d, hh, *_ in chains]
        m1s = [jnp.dot(jnp.concatenate([w_r[0, 0, hh, rows, :], qd_r[0, 0, hh, rows, :]], axis=0),
                       states[n].astype(BF16), preferred_element_type=F32)
               for n, (d, hh, cc, rows, pair_rows, u_r, w_r, qd_r, qk_r, kdt_r, o_r, g_tot) in enumerate(chains)]
        m2s = []
        for n, (d, hh, cc, rows, pair_rows, u_r, w_r, qd_r, qk_r, kdt_r, o_r, g_tot) in enumerate(chains):
            v_new = (u_r[0, 0, hh, rows, :] - m1s[n][:GDN_CHUNK]).astype(BF16)
            v_ext = (jnp.concatenate([v_new, zeros], axis=0) if cc % 2 == 0
                     else jnp.concatenate([zeros, v_new], axis=0))
            lhs2 = jnp.concatenate([qk_r[0, 0, hh, rows, :], kdt_r[0, 0, hh, pair_rows, :]], axis=0)
            m2s.append(jnp.dot(lhs2, v_ext, preferred_element_type=F32))
        for n, (d, hh, cc, rows, pair_rows, u_r, w_r, qd_r, qk_r, kdt_r, o_r, g_tot) in enumerate(chains):
            s_ref[d, hh] = states[n] * g_tot + m2s[n][GDN_CHUNK:]
            o_r[0, rows, hh * GDN_DV:(hh + 1) * GDN_DV] = m1s[n][GDN_CHUNK:] + m2s[n][:GDN_CHUNK]


def _gdn_scan(g_tot, u, w, qd, qk, kdt, n_heads, s):
    bsz, _, nh, t, _ = u.shape
    rb = ROW_BLOCK
    hb = _tile(nh, 8, 1)
    n_blk = t // rb
    n_lat_blk = s // rb

    def spec(d):
        if d == 0:
            return pl.BlockSpec((1, 1, hb, rb, LANES),
                                lambda b, h, i: (b, 0, h, (i + n_lat_blk) % n_blk, 0))
        return pl.BlockSpec((1, 1, hb, rb, LANES), lambda b, h, i: (b, 1, h, n_blk - 1 - i, 0))

    o_shape = jax.ShapeDtypeStruct((bsz, t, nh * GDN_DV), F32)
    return pl.pallas_call(
        functools.partial(_gdn_scan_kernel, n_heads=nh, hb=hb, n_lat_blk=n_lat_blk, n_blk=n_blk),
        grid=(bsz, nh // hb, n_blk),
        in_specs=[pl.BlockSpec(memory_space=pltpu.SMEM)] + [spec(0)] * 5 + [spec(1)] * 5,
        out_specs=[pl.BlockSpec((1, rb, hb * GDN_DV), lambda b, h, i: (b, (i + n_lat_blk) % n_blk, h)),
                   pl.BlockSpec((1, rb, hb * GDN_DV), lambda b, h, i: (b, n_blk - 1 - i, h))],
        out_shape=[o_shape, o_shape],
        scratch_shapes=[pltpu.VMEM((2, hb, GDN_DK, GDN_DV), F32)],
        compiler_params=_params("parallel", "parallel", "arbitrary"),
        name="gdn_scan",
    )(g_tot, u, w, qd, qk, kdt, u, w, qd, qk, kdt)


def _gdn_out_kernel(of_ref, ob_ref, z_ref, nw_ref, o_ref, *, n_heads):
    nw = nw_ref[...]
    for h in range(n_heads):
        cols = slice(h * GDN_DV, (h + 1) * GDN_DV)
        o = of_ref[0, :, cols] + ob_ref[0, :, cols]
        ms = jnp.mean(o * o, axis=-1, keepdims=True)
        o_ref[0, :, cols] = (o * lax.rsqrt(ms + 1e-6) * nw * _silu(z_ref[0, :, cols])).astype(BF16)


def _gdn_out(o_f, o_b, p3, z_col_blk, gdn_norm_w, n_heads, s):
    bsz = o_f.shape[0]
    width = n_heads * GDN_DV
    tm = ROW_BLOCK
    return pl.pallas_call(
        functools.partial(_gdn_out_kernel, n_heads=n_heads),
        grid=(bsz, s // tm),
        in_specs=[pl.BlockSpec((1, tm, width), lambda b, i: (b, i, 0)),
                  pl.BlockSpec((1, tm, width), lambda b, i: (b, i, 0)),
                  pl.BlockSpec((1, tm, width), lambda b, i: (b, i, z_col_blk)),
                  pl.BlockSpec((1, GDN_DV), lambda b, i: (0, 0))],
        out_specs=pl.BlockSpec((1, tm, width), lambda b, i: (b, i, 0)),
        out_shape=jax.ShapeDtypeStruct((bsz, s, width), BF16),
        compiler_params=_params("parallel", "parallel"),
        name="gdn_out",
    )(o_f, o_b, p3, gdn_norm_w.reshape(1, GDN_DV))


def _rope_kernel(q_ref, k_ref, v_ref, cos_ref, sin_ref, qo_ref, ko_ref, vo_ref, *, n_heads):
    cos = cos_ref[...]
    sin = sin_ref[...]
    lane = lax.broadcasted_iota(jnp.int32, cos.shape, 1)
    first_half = (lane % (DIFF_DQK // 2)) < (DIFF_DQK // 4)
    quarter = DIFF_DQK // 4

    def rot(x):
        partner = jnp.where(first_half, pltpu.roll(x, LANES - quarter, axis=1),
                            pltpu.roll(x, quarter, axis=1))
        return x * cos + partner * sin

    for h in range(n_heads):
        cols = slice(h * LANES, (h + 1) * LANES)
        qo_ref[0, h] = (rot(q_ref[0, :, cols]) * Q_SCALE).astype(BF16)
        ko_ref[0, h] = rot(k_ref[0, :, cols]).astype(BF16)
        vo_ref[0, h, :, :DIFF_DV] = v_ref[0, :, cols].astype(BF16)
        vo_ref[0, h, :, DIFF_DV:] = jnp.ones((cos.shape[0], LANES), BF16)


def _rope(p_d, cos_t, sin_t, n_heads):
    bsz, t, _ = p_d.shape
    rb = ROW_BLOCK
    width = n_heads * LANES

    def in_spec(part):
        return pl.BlockSpec((1, rb, width), lambda b, i: (b, i, part))

    tab = pl.BlockSpec((rb, LANES), lambda b, i: (i, 0))
    out = pl.BlockSpec((1, n_heads, rb, LANES), lambda b, i: (b, 0, i, 0))
    shape = jax.ShapeDtypeStruct((bsz, n_heads, t, LANES), BF16)
    return pl.pallas_call(
        functools.partial(_rope_kernel, n_heads=n_heads),
        grid=(bsz, t // rb),
        in_specs=[in_spec(0), in_spec(1), in_spec(2), tab, tab],
        out_specs=[out, out, pl.BlockSpec((1, n_heads, rb, DIFF_DV + LANES), lambda b, i: (b, 0, i, 0))],
        out_shape=[shape, shape,
                   jax.ShapeDtypeStruct((bsz, n_heads, t, DIFF_DV + LANES), BF16)],
        compiler_params=_params("parallel", "parallel"),
        name="diff_rope",
    )(p_d, p_d, p_d, cos_t, sin_t)


def _attn_kernel(q_ref, k_ref, v_ref, lam_ref, nw_ref, o_ref, *, key_chunks):
    q = q_ref[0, 0]
    tq = q.shape[0]
    lane = lax.broadcasted_iota(jnp.int32, q.shape, 1)
    q2 = jnp.concatenate([jnp.where(lane < DIFF_DQK, q, jnp.zeros_like(q)),
                          jnp.where(lane >= DIFF_DQK, q, jnp.zeros_like(q))], axis=0)
    lp = lam_ref[...]
    lam = (jnp.exp(jnp.sum(lp[0:1] * lp[1:2], axis=-1, keepdims=True))
           - jnp.exp(jnp.sum(lp[2:3] * lp[3:4], axis=-1, keepdims=True)) + LAM_INIT)
    m = jnp.full((2 * tq, 1), -jnp.inf, F32)
    acc = jnp.zeros((2 * tq, v_ref.shape[3]), F32)
    for start, size in key_chunks:
        k = k_ref[0, 0, start:start + size, :]
        v = v_ref[0, 0, start:start + size, :]
        s = lax.dot_general(q2, k, _NT, preferred_element_type=F32)
        m_new = jnp.maximum(m, jnp.max(s, axis=-1, keepdims=True))
        p = jnp.exp2(s - m_new).astype(BF16)
        pv = jnp.concatenate([jnp.dot(p[:tq], v, preferred_element_type=F32),
                              jnp.dot(p[tq:], v, preferred_element_type=F32)], axis=0)
        acc = jnp.exp2(m - m_new) * acc + pv
        m = m_new
    pv = acc[:, :DIFF_DV] / acc[:, DIFF_DV:]
    o = pv[:tq] - lam * pv[tq:]
    ms = jnp.mean(o * o, axis=-1, keepdims=True)
    o_ref[0] = (o * lax.rsqrt(ms + 1e-5) * nw_ref[...] * (1.0 - LAM_INIT)).astype(BF16)


def _attention(qr, kr, vr, lam8, diff_norm_w, s):
    bsz, nh, t, _ = kr.shape
    tq = _tile(s, 512, ROW_BLOCK)
    kc = _tile(s, 2048, ROW_BLOCK)
    key_chunks = [(st, kc) for st in range(0, s, kc)] + [(s, t - s)]
    return pl.pallas_call(
        functools.partial(_attn_kernel, key_chunks=tuple(key_chunks)),
        grid=(bsz, nh, s // tq),
        in_specs=[pl.BlockSpec((1, 1, tq, LANES), lambda b, h, i: (b, h, i, 0)),
                  pl.BlockSpec((1, 1, t, LANES), lambda b, h, i: (b, h, 0, 0)),
                  pl.BlockSpec((1, 1, t, vr.shape[3]), lambda b, h, i: (b, h, 0, 0)),
                  pl.BlockSpec((SUBLANES, LANES), lambda b, h, i: (0, 0)),
                  pl.BlockSpec((1, DIFF_DV), lambda b, h, i: (0, 0))],
        out_specs=pl.BlockSpec((1, tq, DIFF_DV), lambda b, h, i: (b, i, h)),
        out_shape=jax.ShapeDtypeStruct((bsz, s, nh * DIFF_DV), BF16),
        compiler_params=_params("parallel", "parallel", "arbitrary"),
        name="diff_attn",
    )(qr, kr, vr, lam8, diff_norm_w.reshape(1, DIFF_DV))


def _merge_kernel(ya_ref, yb_ref, wa_ref, wb_ref, ga_ref, gb_ref, o_ref):
    sa = jax.nn.sigmoid(ga_ref[0])
    sb = jax.nn.sigmoid(gb_ref[0])
    pa = jnp.dot(ya_ref[0], wa_ref[...], preferred_element_type=F32)
    pb = jnp.dot(yb_ref[0], wb_ref[...], preferred_element_type=F32)
    o_ref[0] = (sa * pa + sb * pb).astype(BF16)


def _merge(y_a, y_b, wa, wb, p3, ga_col, gb_col):
    bsz, s, ka = y_a.shape
    kb = y_b.shape[2]
    d = wa.shape[1]
    tm = _tile(s, 512, ROW_BLOCK)
    tn = _tile(math.gcd(d, ga_col, gb_col), 1024, LANES)
    return pl.pallas_call(
        _merge_kernel,
        grid=(bsz, s // tm, d // tn),
        in_specs=[pl.BlockSpec((1, tm, ka), lambda b, i, j: (b, i, 0)),
                  pl.BlockSpec((1, tm, kb), lambda b, i, j: (b, i, 0)),
                  pl.BlockSpec((ka, tn), lambda b, i, j: (0, j)),
                  pl.BlockSpec((kb, tn), lambda b, i, j: (0, j)),
                  pl.BlockSpec((1, tm, tn), lambda b, i, j: (b, i, ga_col // tn + j)),
                  pl.BlockSpec((1, tm, tn), lambda b, i, j: (b, i, gb_col // tn + j))],
        out_specs=pl.BlockSpec((1, tm, tn), lambda b, i, j: (b, i, j)),
        out_shape=jax.ShapeDtypeStruct((bsz, s, d), BF16),
        compiler_params=_params("parallel", "parallel", "arbitrary"),
        name="merge",
    )(y_a, y_b, wa, wb, p3, p3)


LN_ROWS = 32
LN_ACCS = 4


def _layernorm_rows(ref, g_ref, b_ref, part_ref, mean_ref, rstd_ref, emit):
    tm, d = ref.shape
    n_col = d // LANES
    n_grp = tm // LN_ROWS

    def row_slice(r):
        return pl.ds(pl.multiple_of(r * LN_ROWS, LN_ROWS), LN_ROWS)

    def tree_sum(terms):
        accs = list(terms[:LN_ACCS])
        for n, term in enumerate(terms[LN_ACCS:]):
            accs[n % LN_ACCS] = accs[n % LN_ACCS] + term
        while len(accs) > 1:
            accs = [accs[n] + accs[n + 1] for n in range(0, len(accs), 2)]
        return accs[0]

    def sum_body(r, carry):
        rows = row_slice(r)
        part_ref[rows, :] = tree_sum([ref[rows, c * LANES:(c + 1) * LANES] for c in range(n_col)])
        return carry

    lax.fori_loop(0, n_grp, sum_body, 0)
    mean_ref[...] = jnp.broadcast_to(
        jnp.sum(part_ref[...], axis=-1, keepdims=True) * (1.0 / d), mean_ref.shape)

    def var_body(r, carry):
        rows = row_slice(r)
        mu = mean_ref[rows, :]
        sq = []
        for c in range(n_col):
            xc = ref[rows, c * LANES:(c + 1) * LANES] - mu
            sq.append(xc * xc)
        part_ref[rows, :] = tree_sum(sq)
        return carry

    lax.fori_loop(0, n_grp, var_body, 0)
    var = jnp.sum(part_ref[...], axis=-1, keepdims=True) * (1.0 / d)
    rstd_ref[...] = jnp.broadcast_to(lax.rsqrt(var + LN_EPS), rstd_ref.shape)

    def norm_body(r, carry):
        rows = row_slice(r)
        mu = mean_ref[rows, :]
        rs = rstd_ref[rows, :]
        for c in range(n_col):
            cols = slice(c * LANES, (c + 1) * LANES)
            y = (ref[rows, cols] - mu) * rs * g_ref[:, cols] + b_ref[:, cols]
            ref[rows, cols] = y
            emit(rows, cols, y)
        return carry

    lax.fori_loop(0, n_grp, norm_body, 0)


def _proj_ln_kernel(a_ref, w_ref, x_ref, gt_ref, g_ref, b_ref, sc_ref, sh_ref,
                    o_ref, h_ref, part_ref, mean_ref, rstd_ref, mod_ref, *, alpha, n_j, tn,
                    rows_per_batch):
    i = pl.program_id(0)
    j = pl.program_id(1)
    tm = a_ref.shape[0]
    b = (i * tm) // rows_per_batch
    mix = jnp.dot(a_ref[...], w_ref[...], preferred_element_type=F32)
    r = alpha * x_ref[...] + gt_ref[pl.ds(b, 1), :] * mix
    for jj in range(n_j):
        @pl.when(j == jj)
        def _(jj=jj):
            o_ref[:, jj * tn:(jj + 1) * tn] = r

    @pl.when(j == n_j - 1)
    def _():
        if h_ref is None:
            emit = lambda rows, cols, y: None
        else:
            mod_ref[0:1, :] = 1.0 + sc_ref[pl.ds(b, 1), :]
            mod_ref[1:2, :] = sh_ref[pl.ds(b, 1), :]

            def emit(rows, cols, y):
                h_ref[rows, cols] = (y * mod_ref[0:1, cols] + mod_ref[1:2, cols]).astype(BF16)

        _layernorm_rows(o_ref, g_ref, b_ref, part_ref, mean_ref, rstd_ref, emit)


def _proj_ln_kernel_noh(a_ref, w_ref, x_ref, gt_ref, g_ref, b_ref, o_ref,
                        part_ref, mean_ref, rstd_ref, mod_ref, **kw):
    _proj_ln_kernel(a_ref, w_ref, x_ref, gt_ref, g_ref, b_ref, None, None, o_ref, None,
                    part_ref, mean_ref, rstd_ref, mod_ref, **kw)


def _proj_ln(a, w, x, mod, gt_chunk, ln_g, ln_b, rows_per_batch, *, alpha, tm, tn, name,
             next_mod_chunks=None):
    m, k = a.shape
    d = w.shape[1]
    n_j = d // tn
    kw = dict(alpha=alpha, n_j=n_j, tn=tn, rows_per_batch=rows_per_batch)
    in_specs = [pl.BlockSpec((tm, k), lambda i, j: (i, 0)),
                pl.BlockSpec((k, tn), lambda i, j: (0, j)),
                pl.BlockSpec((tm, tn), lambda i, j: (i, j)),
                pl.BlockSpec((SUBLANES, tn), lambda i, j: (0, gt_chunk * n_j + j)),
                pl.BlockSpec((1, d), lambda i, j: (0, 0)),
                pl.BlockSpec((1, d), lambda i, j: (0, 0))]
    args = [a, w, x, mod, ln_g.reshape(1, d), ln_b.reshape(1, d)]
    row_spec = pl.BlockSpec((tm, d), lambda i, j: (i, 0))
    if next_mod_chunks is None:
        kernel = functools.partial(_proj_ln_kernel_noh, **kw)
        out_specs = row_spec
        out_shape = jax.ShapeDtypeStruct((m, d), F32)
    else:
        sh_chunk, sc_chunk = next_mod_chunks
        kernel = functools.partial(_proj_ln_kernel, **kw)
        in_specs += [pl.BlockSpec((SUBLANES, d), lambda i, j: (0, sc_chunk)),
                     pl.BlockSpec((SUBLANES, d), lambda i, j: (0, sh_chunk))]
        args += [mod, mod]
        out_specs = [row_spec, row_spec]
        out_shape = [jax.ShapeDtypeStruct((m, d), F32), jax.ShapeDtypeStruct((m, d), BF16)]
    return pl.pallas_call(
        kernel,
        grid=(m // tm, n_j),
        in_specs=in_specs,
        out_specs=out_specs,
        out_shape=out_shape,
        scratch_shapes=[pltpu.VMEM((tm, LANES), F32), pltpu.VMEM((tm, LANES), F32),
                        pltpu.VMEM((tm, LANES), F32), pltpu.VMEM((SUBLANES, d), F32)],
        compiler_params=_params("parallel", "arbitrary"),
        name=name,
    )(*args)


def _ffn_up_kernel(h_ref, wg_ref, wu_ref, o_ref):
    h = h_ref[...]
    g = jnp.dot(h, wg_ref[...].astype(BF16), preferred_element_type=F32)
    u = jnp.dot(h, wu_ref[...].astype(BF16), preferred_element_type=F32)
    o_ref[...] = (_silu(g) * u).astype(BF16)


def _ffn_up(h, wg, wu):
    m, d = h.shape
    f = wg.shape[1]
    tm = _tile(m, 2048, ROW_BLOCK)
    tn = _tile(f, 512, LANES)
    return pl.pallas_call(
        _ffn_up_kernel,
        grid=(m // tm, f // tn),
        in_specs=[pl.BlockSpec((tm, d), lambda i, j: (i, 0)),
                  pl.BlockSpec((d, tn), lambda i, j: (0, j)),
                  pl.BlockSpec((d, tn), lambda i, j: (0, j))],
        out_specs=pl.BlockSpec((tm, tn), lambda i, j: (i, j)),
        out_shape=jax.ShapeDtypeStruct((m, f), BF16),
        compiler_params=_params("parallel", "arbitrary"),
        name="ffn_up",
    )(h, wg, wu)


def _rope_tables(s, c):
    quarter = DIFF_DQK // 4
    inv_freq = ROPE_BASE ** (-jnp.arange(quarter, dtype=F32) / quarter)
    pos = jnp.arange(s, dtype=jnp.int32)
    lane = jnp.arange(LANES)
    use_col = (lane % DIFF_DQK) >= (DIFF_DQK // 2)
    p = jnp.where(use_col[None, :], (pos % GRID_W)[:, None], (pos // GRID_W)[:, None]).astype(F32)
    ang = p * inv_freq[lane % quarter][None, :]
    sign = jnp.where((lane % (2 * quarter)) < quarter, -1.0, 1.0)[None, :]
    cos = jnp.concatenate([jnp.cos(ang), jnp.ones((c, LANES), F32)], axis=0)
    sin = jnp.concatenate([jnp.sin(ang) * sign, jnp.zeros((c, LANES), F32)], axis=0)
    return cos, sin


def kernel(x, c, ctx, c_ctx, w_ada, b_ada, w_in, conv_w, a_log, dt_bias, gdn_norm_w, lam_q1, lam_k1,
           lam_q2, lam_k2, diff_norm_w, w_proj_a, w_proj_b, w_out, ln1_g, ln1_b, w_gate, w_up, w_down,
           ln2_g, ln2_b):
    bsz, s, d = x.shape
    cl = ctx.shape[1]
    t = s + cl
    nh = d // HEAD_W
    qk_w = nh * GDN_DK
    v_w = nh * GDN_DV
    dqk_w = nh * 2 * DIFF_DQK
    dv_w = nh * DIFF_DV
    alpha = (2.0 * w_ada.shape[0]) ** 0.25
    assert w_ada.shape[0] == 1 and s % ROW_BLOCK == 0 and cl % ROW_BLOCK == 0 and 8 * nh <= LANES

    c8 = jnp.zeros((SUBLANES, d), F32).at[:bsz].set(c).at[bsz].set(c_ctx)
    mod = _ada(c8, w_ada[0], b_ada[0])

    h_all = _modulate(x, ctx, mod, bsz)

    w_l = jnp.swapaxes(w_in[0], 0, 1)
    z_col = 2 * qk_w + v_w
    ab_start = z_col + v_w
    d_start = ab_start + 4 * nh
    n_d = 2 * dqk_w + dv_w + 2 * d
    h2d = h_all.reshape(bsz * t, d)
    tm_in = _tile(bsz * t, 512, ROW_BLOCK)
    p_g = _in_proj(h2d, w_l, 0, ab_start, tm=tm_in, tn=_tile(ab_start, 1024, LANES),
                   name="in_proj_gdn").reshape(bsz, t, ab_start)
    p_ab = _in_proj(h2d, w_l, ab_start, LANES, tm=tm_in, tn=LANES,
                    name="in_proj_ab").reshape(bsz, t, LANES)
    p_d = _in_proj(h2d, w_l, d_start, n_d, tm=tm_in, tn=_tile(n_d, 1024, LANES),
                   name="in_proj_diff").reshape(bsz, t, n_d)
    ga_col = 2 * dqk_w + dv_w
    gb_col = ga_col + d

    w2 = 2 * nh
    alog128 = jnp.pad(a_log[0].reshape(1, w2), ((0, 0), (0, LANES - w2)))
    dtb128 = jnp.pad(dt_bias[0].reshape(1, w2), ((0, 0), (0, LANES - w2)))
    gb = _gates(p_ab, alog128, dtb128, w2)
    gr = jnp.swapaxes(gb[:, :, :w2], 1, 2)
    g_tot = gb[:, ::GDN_CHUNK, 3 * w2:4 * w2].reshape(-1)
    qkv = _gdn_conv(p_g, conv_w[0], nh, s)
    u, w_, qd, qk, kdt = _gdn_intra(qkv, gb, gr, nh)
    o_f, o_b = _gdn_scan(g_tot, u, w_, qd, qk, kdt, nh, s)
    y_a = _gdn_out(o_f, o_b, p_g, z_col // v_w, gdn_norm_w[0], nh, s)

    cos_t, sin_t = _rope_tables(s, cl)
    qr, kr, vr = _rope(p_d, cos_t, sin_t, nh)
    lam8 = jnp.zeros((SUBLANES, LANES), F32)
    for r, vec in enumerate((lam_q1, lam_k1, lam_q2, lam_k2)):
        lam8 = lam8.at[r, :DIFF_DQK].set(vec[0])
    y_b = _attention(qr, kr, vr, lam8, diff_norm_w[0], s)

    ymg = _merge(y_a, y_b, w_proj_a[0].astype(BF16), w_proj_b[0].astype(BF16), p_d, ga_col, gb_col)
    m = bsz * s
    x1, h2 = _proj_ln(ymg.reshape(m, d), w_out[0].astype(BF16), x.reshape(m, d), mod, 2,
                      ln1_g[0], ln1_b[0], s, alpha=alpha, tm=_tile(s, 512, ROW_BLOCK),
                      tn=_tile(d, 512, LANES), name="out_proj_ln", next_mod_chunks=(3, 4))

    hff = _ffn_up(h2, w_gate[0], w_up[0])
    out = _proj_ln(hff, w_down[0].astype(BF16), x1, mod, 5, ln2_g[0], ln2_b[0], s, alpha=alpha,
                   tm=_tile(s, 512, ROW_BLOCK), tn=_tile(d, 256, LANES), name="ffn_down_ln")
    return out.reshape(bsz, s, d)
```

```python
import functools
import math

import jax
import jax.numpy as jnp
from jax import lax
from jax.experimental import pallas as pl
from jax.experimental.pallas import tpu as pltpu

F32 = jnp.float32
BF16 = jnp.bfloat16

GRID_W = 64
HEAD_W = 256
GDN_DK = 128
GDN_DV = 128
GDN_CHUNK = 64
DIFF_DQK = 64
DIFF_DV = 128
ROPE_BASE = 10000.0
LN_EPS = 1e-6
LAM_INIT = 0.8 - 0.6 * math.exp(-0.3 * 0)
Q_SCALE = DIFF_DQK ** -0.5 * math.log2(math.e)

LANES = 128
SUBLANES = 8
PAIR = 2 * GDN_CHUNK
ROW_BLOCK = 256
VMEM_LIMIT = 56 * 1024 * 1024

_NT = (((1,), (1,)), ((), ()))


def _params(*sem):
    return pltpu.CompilerParams(dimension_semantics=sem, vmem_limit_bytes=VMEM_LIMIT)


def _tile(n, target, quantum):
    best = None
    t = quantum
    while t <= min(n, target):
        if n % t == 0:
            best = t
        t += quantum
    assert best is not None, (n, target, quantum)
    return best


def _silu(x):
    return x * jax.nn.sigmoid(x)


def _ada_kernel(c_ref, w_ref, b_ref, o_ref):
    c = _silu(c_ref[...])
    o_ref[...] = jnp.dot(c.astype(BF16), w_ref[...].astype(BF16),
                         preferred_element_type=F32) + b_ref[...]


def _ada(c8, w_ada, b_ada):
    d, n = w_ada.shape
    tn = _tile(n, 512, LANES)
    return pl.pallas_call(
        _ada_kernel,
        grid=(n // tn,),
        in_specs=[pl.BlockSpec((SUBLANES, d), lambda j: (0, 0)),
                  pl.BlockSpec((d, tn), lambda j: (0, j)),
                  pl.BlockSpec((1, tn), lambda j: (0, j))],
        out_specs=pl.BlockSpec((SUBLANES, tn), lambda j: (0, j)),
        out_shape=jax.ShapeDtypeStruct((SUBLANES, n), F32),
        compiler_params=_params("parallel"),
        name="ada",
    )(c8, w_ada, b_ada.reshape(1, n))


def _modulate_kernel(x_ref, ctx_ref, sh_ref, sc_ref, o_ref, *, n_lat, ctx_row):
    b = pl.program_id(0)
    i = pl.program_id(1)

    @pl.when(i < n_lat)
    def _():
        sc = sc_ref[pl.ds(b, 1), :]
        sh = sh_ref[pl.ds(b, 1), :]
        o_ref[0] = (x_ref[0] * (1.0 + sc) + sh).astype(BF16)

    @pl.when(i >= n_lat)
    def _():
        sc = sc_ref[ctx_row:ctx_row + 1, :]
        sh = sh_ref[ctx_row:ctx_row + 1, :]
        o_ref[0] = (ctx_ref[0] * (1.0 + sc) + sh).astype(BF16)


def _modulate(x, ctx, mod, ctx_row):
    bsz, s, d = x.shape
    c = ctx.shape[1]
    t = s + c
    rb = ROW_BLOCK
    n_lat = s // rb
    return pl.pallas_call(
        functools.partial(_modulate_kernel, n_lat=n_lat, ctx_row=ctx_row),
        grid=(bsz, t // rb),
        in_specs=[pl.BlockSpec((1, rb, d), lambda b, i: (b, jnp.minimum(i, n_lat - 1), 0)),
                  pl.BlockSpec((1, rb, d), lambda b, i: (b, jnp.maximum(i - n_lat, 0), 0)),
                  pl.BlockSpec((SUBLANES, d), lambda b, i: (0, 0)),
                  pl.BlockSpec((SUBLANES, d), lambda b, i: (0, 1))],
        out_specs=pl.BlockSpec((1, rb, d), lambda b, i: (b, i, 0)),
        out_shape=jax.ShapeDtypeStruct((bsz, t, d), BF16),
        compiler_params=_params("parallel", "arbitrary"),
        name="modulate",
    )(x, ctx, mod, mod)


def _in_proj_kernel(a_ref, w_hbm, o_ref, wb_ref, stage_ref, sem, *, row0, n_tiles, n_chunks):
    j = pl.program_id(0)
    i = pl.program_id(1)
    tn = wb_ref.shape[1]
    ch = tn // n_chunks

    def chunk_copy(tile, c, slot):
        rows = pl.ds(pl.multiple_of(row0 + tile * tn + c * ch, SUBLANES), ch)
        return pltpu.make_async_copy(w_hbm.at[rows, :], stage_ref.at[slot], sem.at[slot])

    @pl.when(jnp.logical_and(j == 0, i == 0))
    def _():
        chunk_copy(0, 0, 0).start()
        for c in range(n_chunks):
            if c + 1 < n_chunks:
                chunk_copy(0, c + 1, (c + 1) % 2).start()
            chunk_copy(0, c, c % 2).wait()
            wb_ref[0, c * ch:(c + 1) * ch, :] = stage_ref[c % 2].astype(BF16)

    has_next = j + 1 < n_tiles

    @pl.when(jnp.logical_and(has_next, i < n_chunks))
    def _():
        chunk_copy(j + 1, i, i % 2).start()

    @pl.when(jnp.logical_and(has_next, jnp.logical_and(i >= 1, i <= n_chunks)))
    def _():
        c = i - 1
        chunk_copy(j + 1, c, c % 2).wait()
        wb_ref[(j + 1) % 2, pl.ds(pl.multiple_of(c * ch, ch), ch), :] = stage_ref[c % 2].astype(BF16)

    o_ref[...] = lax.dot_general(a_ref[...], wb_ref[j % 2], _NT, preferred_element_type=F32)


def _in_proj(a, w_t, row0, n, *, tm, tn, name):
    m, k = a.shape
    n_tiles = n // tn
    n_row_tiles = m // tm
    assert n % tn == 0 and row0 % SUBLANES == 0 and n_row_tiles >= 2
    n_chunks = max(c for c in range(1, tn // LANES + 1)
                   if (tn // LANES) % c == 0 and c <= n_row_tiles - 1)
    return pl.pallas_call(
        functools.partial(_in_proj_kernel, row0=row0, n_tiles=n_tiles, n_chunks=n_chunks),
        grid=(n_tiles, n_row_tiles),
        in_specs=[pl.BlockSpec((tm, k), lambda j, i: (i, 0)),
                  pl.BlockSpec(memory_space=pl.ANY)],
        out_specs=pl.BlockSpec((tm, tn), lambda j, i: (i, j)),
        out_shape=jax.ShapeDtypeStruct((m, n), F32),
        scratch_shapes=[pltpu.VMEM((2, tn, k), BF16),
                        pltpu.VMEM((2, tn // n_chunks, k), F32),
                        pltpu.SemaphoreType.DMA((2,))],
        compiler_params=_params("arbitrary", "arbitrary"),
        name=name,
    )(a, w_t)


def _gates_kernel(p_ref, alog_ref, dtb_ref, o_ref, *, w2):
    p = p_ref[0]
    rb = p.shape[0]
    lane = lax.broadcasted_iota(jnp.int32, p.shape, 1)
    row = lax.broadcasted_iota(jnp.int32, p.shape, 0) % GDN_CHUNK
    xa = p + dtb_ref[...]
    softplus = jnp.maximum(xa, 0.0) + jnp.log(1.0 + jnp.exp(-jnp.abs(xa)))
    g = jnp.where(lane < w2, -jnp.exp(alog_ref[...]) * softplus, 0.0)
    pre = g
    suf = g
    step = 1
    while step < GDN_CHUNK:
        pre = pre + jnp.where(row >= step, pltpu.roll(pre, step, axis=0), 0.0)
        suf = suf + jnp.where(row < GDN_CHUNK - step, pltpu.roll(suf, rb - step, axis=0), 0.0)
        step *= 2
    tot = pre + suf - g
    cum = jnp.where(lane < w2 // 2, pre, suf)
    beta = jax.nn.sigmoid(p)
    out = jnp.where(lane < w2, cum,
                    jnp.where(lane < 2 * w2, beta,
                              jnp.where(lane < 3 * w2, pltpu.roll(tot, 2 * w2, axis=1),
                                        pltpu.roll(jnp.exp(tot), 3 * w2, axis=1))))
    o_ref[0] = jnp.where(lane < 4 * w2, out, 0.0)


def _gates(p_ab, alog128, dtb128, w2):
    bsz, t, _ = p_ab.shape
    rb = ROW_BLOCK
    return pl.pallas_call(
        functools.partial(_gates_kernel, w2=w2),
        grid=(bsz, t // rb),
        in_specs=[pl.BlockSpec((1, rb, LANES), lambda b, i: (b, i, 0)),
                  pl.BlockSpec((1, LANES), lambda b, i: (0, 0)),
                  pl.BlockSpec((1, LANES), lambda b, i: (0, 0))],
        out_specs=pl.BlockSpec((1, rb, LANES), lambda b, i: (b, i, 0)),
        out_shape=jax.ShapeDtypeStruct((bsz, t, LANES), F32),
        compiler_params=_params("parallel", "parallel"),
        name="gdn_gates",
    )(p_ab, alog128, dtb128)


def _conv_kernel(x_ref, cw_ref, o_ref, *, n_heads, cb, n_lat, n_blk, taps):
    j = pl.program_id(1)
    t = x_ref.shape[1]
    width = x_ref.shape[2]
    rb = ROW_BLOCK
    halo = SUBLANES
    pad = taps // 2
    win_rows = rb + 2 * halo
    w = cw_ref[...]
    is_qk = j * cb < 2 * n_heads
    q_scale = jnp.where(j * cb < n_heads, GDN_DK ** -0.5, 1.0)

    def body(r, carry):
        base = pl.multiple_of(r * rb, rb)
        top_ok = jnp.logical_and(r != 0, r != n_lat)
        bot_ok = jnp.logical_and(r != n_lat - 1, r != n_blk - 1)
        top = x_ref[0, pl.ds(pl.multiple_of(jnp.maximum(base - halo, 0), halo), halo), :]
        bot = x_ref[0, pl.ds(pl.multiple_of(jnp.minimum(base + rb, t - halo), halo), halo), :]
        win = jnp.concatenate([jnp.where(top_ok, top, 0.0), x_ref[0, pl.ds(base, rb), :],
                               jnp.where(bot_ok, bot, 0.0)], axis=0)
        acc = jnp.zeros((win_rows, width), F32)
        for tap in range(taps):
            shift = (pad - tap) % win_rows
            shifted = win if shift == 0 else pltpu.roll(win, shift, axis=0)
            acc = acc + shifted * w[tap:tap + 1, :]
        y = _silu(acc[halo:halo + rb, :])
        for c in range(cb):
            yc = y[:, c * LANES:(c + 1) * LANES]
            ss = jnp.sum(yc * yc, axis=-1, keepdims=True)
            scale = jnp.where(is_qk, lax.rsqrt(ss + 1e-6) * q_scale, 1.0)
            o_ref[0, c, pl.ds(base, rb), :] = yc * scale
        return carry

    lax.fori_loop(0, n_blk, body, 0)


def _gdn_conv(p_g, conv_w, n_heads, s):
    bsz, t, _ = p_g.shape
    taps = conv_w.shape[0]
    n_cols = 3 * n_heads
    cb = _tile(n_heads, 4, 1)
    return pl.pallas_call(
        functools.partial(_conv_kernel, n_heads=n_heads, cb=cb, n_lat=s // ROW_BLOCK,
                          n_blk=t // ROW_BLOCK, taps=taps),
        grid=(bsz, n_cols // cb),
        in_specs=[pl.BlockSpec((1, t, cb * LANES), lambda b, j: (b, 0, j)),
                  pl.BlockSpec((taps, cb * LANES), lambda b, j: (0, j))],
        out_specs=pl.BlockSpec((1, cb, t, LANES), lambda b, j: (b, j, 0, 0)),
        out_shape=jax.ShapeDtypeStruct((bsz, n_cols, t, LANES), F32),
        compiler_params=_params("parallel", "parallel"),
        name="gdn_conv",
    )(p_g, conv_w)


def _gdn_intra_kernel(q_ref, k_ref, v_ref, gb_ref, gr_ref,
                      u_ref, w_ref, qd_ref, qk_ref, kdt_ref, *, n_heads, hb):
    hblk = pl.program_id(1)
    w2 = 2 * n_heads
    rb = q_ref.shape[2]
    gbv = gb_ref[0]
    lane = lax.broadcasted_iota(jnp.int32, gbv.shape, 1)

    def column(idx):
        return jnp.sum(jnp.where(lane == idx, gbv, 0.0), axis=-1, keepdims=True)

    ri = lax.broadcasted_iota(jnp.int32, (PAIR, PAIR), 0)
    ci = lax.broadcasted_iota(jnp.int32, (PAIR, PAIR), 1)
    same = (ri >= GDN_CHUNK) == (ci >= GDN_CHUNK)
    eye_f32 = jnp.where(ri == ci, 1.0, 0.0)
    levels = int(math.log2(GDN_CHUNK))
    off_blocks = ([], [])
    for lvl in range(levels):
        half = 1 << lvl
        same_blk = (ri >> (lvl + 1)) == (ci >> (lvl + 1))
        r_hi = (ri & half) != 0
        c_hi = (ci & half) != 0
        off_blocks[0].append(same_blk & r_hi & jnp.logical_not(c_hi))
        off_blocks[1].append(same_blk & c_hi & jnp.logical_not(r_hi))

    incls = (jnp.logical_and(same, ri >= ci), jnp.logical_and(same, ri <= ci))
    stricts = (jnp.logical_and(same, ri > ci), jnp.logical_and(same, ri < ci))

    probs = []
    for hh in range(hb):
        head = hblk * hb + hh
        gcols = [column(d * n_heads + head) for d in range(2)]
        bcols = [column(w2 + d * n_heads + head) for d in range(2)]
        tcols = [column(2 * w2 + d * n_heads + head) for d in range(2)]
        grows = [gr_ref[0, pl.ds(d * n_heads + head, 1), :] for d in range(2)]
        for p in range(rb // PAIR):
            rows = slice(p * PAIR, (p + 1) * PAIR)
            q = q_ref[0, hh, rows, :]
            k = k_ref[0, hh, rows, :]
            kbf = k.astype(BF16)
            kk = lax.dot_general(kbf, kbf, _NT, preferred_element_type=F32)
            qkm = lax.dot_general(q.astype(BF16), kbf, _NT, preferred_element_type=F32)
            for d in range(2):
                gc = gcols[d][rows]
                bc = bcols[d][rows]
                dec = jnp.where(incls[d], jnp.exp(jnp.where(incls[d], gc - grows[d][:, rows], 0.0)), 0.0)
                a = jnp.where(stricts[d], kk * bc * dec, 0.0)
                qk_ref[0, d, hh, rows, :] = jnp.where(incls[d], qkm * dec, 0.0).astype(BF16)
                probs.append(dict(hh=hh, rows=rows, d=d, a=a, gc=gc, bc=bc, tc=tcols[d][rows]))

    minv = [eye_f32 - jnp.where(off_blocks[pr["d"]][0], pr["a"], 0.0) for pr in probs]
    a_bf = [pr["a"].astype(BF16) for pr in probs]
    for lvl in range(1, levels):
        mb = [m.astype(BF16) for m in minv]
        t1 = [jnp.dot(a_bf[i], mb[i], preferred_element_type=F32).astype(BF16)
              for i in range(len(probs))]
        minv = [minv[i] - jnp.where(off_blocks[pr["d"]][lvl],
                                    jnp.dot(mb[i], t1[i], preferred_element_type=F32), 0.0)
                for i, pr in enumerate(probs)]

    for i, pr in enumerate(probs):
        hh, rows, d, gc, bc = pr["hh"], pr["rows"], pr["d"], pr["gc"], pr["bc"]
        q = q_ref[0, hh, rows, :]
        k = k_ref[0, hh, rows, :]
        v = v_ref[0, hh, rows, :]
        eg = jnp.exp(gc)
        vb = v * bc
        kbg = k * (bc * eg)
        rhs = jnp.concatenate([vb, kbg], axis=1).astype(BF16)
        uw = jnp.dot((minv[i] - eye_f32).astype(BF16), rhs, preferred_element_type=F32)
        u_ref[0, d, hh, rows, :] = vb + uw[:, :GDN_DV]
        w_ref[0, d, hh, rows, :] = (kbg + uw[:, GDN_DV:]).astype(BF16)
        qd_ref[0, d, hh, rows, :] = (q * eg).astype(BF16)
        kdt_ref[0, d, hh, rows, :] = (k * jnp.exp(pr["tc"] - gc)).T.astype(BF16)


def _gdn_intra(qkv, gb, gr, n_heads):
    bsz, _, t, _ = qkv.shape
    rb = ROW_BLOCK
    nh = n_heads
    hb = _tile(nh, 4, 1)
    nhb = nh // hb

    def qkv_spec(part):
        return pl.BlockSpec((1, hb, rb, LANES), lambda b, h, i: (b, part * nhb + h, i, 0))

    out_spec = pl.BlockSpec((1, 2, hb, rb, LANES), lambda b, h, i: (b, 0, h, i, 0))
    shape = (bsz, 2, nh, t, LANES)
    return pl.pallas_call(
        functools.partial(_gdn_intra_kernel, n_heads=nh, hb=hb),
        grid=(bsz, nhb, t // rb),
        in_specs=[qkv_spec(0), qkv_spec(1), qkv_spec(2),
                  pl.BlockSpec((1, rb, LANES), lambda b, h, i: (b, i, 0)),
                  pl.BlockSpec((1, gr.shape[1], rb), lambda b, h, i: (b, 0, i))],
        out_specs=[out_spec] * 5,
        out_shape=[jax.ShapeDtypeStruct(shape, F32)] + [jax.ShapeDtypeStruct(shape, BF16)] * 4,
        compiler_params=_params("parallel", "parallel", "parallel"),
        name="gdn_intra",
    )(qkv, qkv, qkv, gb, gr)


def _gdn_scan_kernel(gt_ref, uf, wf, qdf, qkf, kdtf, ub, wb, qdb, qkb, kdtb,
                     of_ref, ob_ref, s_ref, *, n_heads, hb, n_lat_blk, n_blk):
    b = pl.program_id(0)
    hblk = pl.program_id(1)
    i = pl.program_id(2)
    rb = uf.shape[3]
    n_ch = rb // GDN_CHUNK
    w2 = 2 * n_heads

    @pl.when(i == 0)
    def _():
        s_ref[...] = jnp.zeros(s_ref.shape, F32)

    blk_f = (i + n_lat_blk) % n_blk
    blk_b = n_blk - 1 - i
    zeros = jnp.zeros((GDN_CHUNK, GDN_DV), BF16)
    dirs = ((0, uf, wf, qdf, qkf, kdtf, of_ref, blk_f),
            (1, ub, wb, qdb, qkb, kdtb, ob_ref, blk_b))
    for c in range(n_ch):
        chains = []
        for d, u_r, w_r, qd_r, qk_r, kdt_r, o_r, blk in dirs:
            cc = c if d == 0 else n_ch - 1 - c
            rows = slice(cc * GDN_CHUNK, (cc + 1) * GDN_CHUNK)
            pair_rows = slice((cc // 2) * PAIR, (cc // 2 + 1) * PAIR)
            for hh in range(hb):
                g_tot = gt_ref[(b * (n_blk * n_ch) + blk * n_ch + cc) * w2 + d * n_heads + hblk * hb + hh]
                chains.append((d, hh, cc, rows, pair_rows, u_r, w_r, qd_r, qk_r, kdt_r, o_r, g_tot))
        states = [s_ref[d, hh] for d, hh, *_ in chains]
        m1s = [jnp.dot(jnp.concatenate([w_r[0, 0, hh, rows, :], qd_r[0, 0, hh, rows, :]], axis=0),
                       states[n].astype(BF16), preferred_element_type=F32)
               for n, (d, hh, cc, rows, pair_rows, u_r, w_r, qd_r, qk_r, kdt_r, o_r, g_tot) in enumerate(chains)]
        m2s = []
        for n, (d, hh, cc, rows, pair_rows, u_r, w_r, qd_r, qk_r, kdt_r, o_r, g_tot) in enumerate(chains):
            v_new = (u_r[0, 0, hh, rows, :] - m1s[n][:GDN_CHUNK]).astype(BF16)
            v_ext = (jnp.concatenate([v_new, zeros], axis=0) if cc % 2 == 0
                     else jnp.concatenate([zeros, v_new], axis=0))
            lhs2 = jnp.concatenate([qk_r[0, 0, hh, rows, :], kdt_r[0, 0, hh, pair_rows, :]], axis=0)
            m2s.append(jnp.dot(lhs2, v_ext, preferred_element_type=F32))
        for n, (d, hh, cc, rows, pair_rows, u_r, w_r, qd_r, qk_r, kdt_r, o_r, g_tot) in enumerate(chains):
            s_ref[d, hh] = states[n] * g_tot + m2s[n][GDN_CHUNK:]
            o_r[0, rows, hh * GDN_DV:(hh + 1) * GDN_DV] = m1s[n][GDN_CHUNK:] + m2s[n][:GDN_CHUNK]


def _gdn_scan(g_tot, u, w, qd, qk, kdt, n_heads, s):
    bsz, _, nh, t, _ = u.shape
    rb = ROW_BLOCK
    hb = _tile(nh, 8, 1)
    n_blk = t // rb
    n_lat_blk = s // rb

    def spec(d):
        if d == 0:
            return pl.BlockSpec((1, 1, hb, rb, LANES),
                                lambda b, h, i: (b, 0, h, (i + n_lat_blk) % n_blk, 0))
        return pl.BlockSpec((1, 1, hb, rb, LANES), lambda b, h, i: (b, 1, h, n_blk - 1 - i, 0))

    o_shape = jax.ShapeDtypeStruct((bsz, t, nh * GDN_DV), F32)
    return pl.pallas_call(
        functools.partial(_gdn_scan_kernel, n_heads=nh, hb=hb, n_lat_blk=n_lat_blk, n_blk=n_blk),
        grid=(bsz, nh // hb, n_blk),
        in_specs=[pl.BlockSpec(memory_space=pltpu.SMEM)] + [spec(0)] * 5 + [spec(1)] * 5,
        out_specs=[pl.BlockSpec((1, rb, hb * GDN_DV), lambda b, h, i: (b, (i + n_lat_blk) % n_blk, h)),
                   pl.BlockSpec((1, rb, hb * GDN_DV), lambda b, h, i: (b, n_blk - 1 - i, h))],
        out_shape=[o_shape, o_shape],
        scratch_shapes=[pltpu.VMEM((2, hb, GDN_DK, GDN_DV), F32)],
        compiler_params=_params("parallel", "parallel", "arbitrary"),
        name="gdn_scan",
    )(g_tot, u, w, qd, qk, kdt, u, w, qd, qk, kdt)


def _gdn_out_kernel(of_ref, ob_ref, z_ref, nw_ref, o_ref, *, n_heads):
    nw = nw_ref[...]
    for h in range(n_heads):
        cols = slice(h * GDN_DV, (h + 1) * GDN_DV)
        o = of_ref[0, :, cols] + ob_ref[0, :, cols]
        ms = jnp.mean(o * o, axis=-1, keepdims=True)
        o_ref[0, :, cols] = (o * lax.rsqrt(ms + 1e-6) * nw * _silu(z_ref[0, :, cols])).astype(BF16)


def _gdn_out(o_f, o_b, p3, z_col_blk, gdn_norm_w, n_heads, s):
    bsz = o_f.shape[0]
    width = n_heads * GDN_DV
    tm = ROW_BLOCK
    return pl.pallas_call(
        functools.partial(_gdn_out_kernel, n_heads=n_heads),
        grid=(bsz, s // tm),
        in_specs=[pl.BlockSpec((1, tm, width), lambda b, i: (b, i, 0)),
                  pl.BlockSpec((1, tm, width), lambda b, i: (b, i, 0)),
                  pl.BlockSpec((1, tm, width), lambda b, i: (b, i, z_col_blk)),
                  pl.BlockSpec((1, GDN_DV), lambda b, i: (0, 0))],
        out_specs=pl.BlockSpec((1, tm, width), lambda b, i: (b, i, 0)),
        out_shape=jax.ShapeDtypeStruct((bsz, s, width), BF16),
        compiler_params=_params("parallel", "parallel"),
        name="gdn_out",
    )(o_f, o_b, p3, gdn_norm_w.reshape(1, GDN_DV))


def _rope_kernel(q_ref, k_ref, v_ref, cos_ref, sin_ref, qo_ref, ko_ref, vo_ref, *, n_heads):
    cos = cos_ref[...]
    sin = sin_ref[...]
    lane = lax.broadcasted_iota(jnp.int32, cos.shape, 1)
    first_half = (lane % (DIFF_DQK // 2)) < (DIFF_DQK // 4)
    quarter = DIFF_DQK // 4

    def rot(x):
        partner = jnp.where(first_half, pltpu.roll(x, LANES - quarter, axis=1),
                            pltpu.roll(x, quarter, axis=1))
        return x * cos + partner * sin

    for h in range(n_heads):
        cols = slice(h * LANES, (h + 1) * LANES)
        qo_ref[0, h] = (rot(q_ref[0, :, cols]) * Q_SCALE).astype(BF16)
        ko_ref[0, h] = rot(k_ref[0, :, cols]).astype(BF16)
        vo_ref[0, h] = v_ref[0, :, cols].astype(BF16)


def _rope(p_d, cos_t, sin_t, n_heads):
    bsz, t, _ = p_d.shape
    rb = ROW_BLOCK
    width = n_heads * LANES

    def in_spec(part):
        return pl.BlockSpec((1, rb, width), lambda b, i: (b, i, part))

    tab = pl.BlockSpec((rb, LANES), lambda b, i: (i, 0))
    out = pl.BlockSpec((1, n_heads, rb, LANES), lambda b, i: (b, 0, i, 0))
    shape = jax.ShapeDtypeStruct((bsz, n_heads, t, LANES), BF16)
    return pl.pallas_call(
        functools.partial(_rope_kernel, n_heads=n_heads),
        grid=(bsz, t // rb),
        in_specs=[in_spec(0), in_spec(1), in_spec(2), tab, tab],
        out_specs=[out, out, out],
        out_shape=[shape, shape, shape],
        compiler_params=_params("parallel", "parallel"),
        name="diff_rope",
    )(p_d, p_d, p_d, cos_t, sin_t)


def _attn_kernel(q_ref, k_ref, v_ref, lam_ref, nw_ref, o_ref, *, key_chunks):
    q = q_ref[0, 0]
    tq = q.shape[0]
    lane = lax.broadcasted_iota(jnp.int32, q.shape, 1)
    q2 = jnp.concatenate([jnp.where(lane < DIFF_DQK, q, jnp.zeros_like(q)),
                          jnp.where(lane >= DIFF_DQK, q, jnp.zeros_like(q))], axis=0)
    lp = lam_ref[...]
    lam = (jnp.exp(jnp.sum(lp[0:1] * lp[1:2], axis=-1, keepdims=True))
           - jnp.exp(jnp.sum(lp[2:3] * lp[3:4], axis=-1, keepdims=True)) + LAM_INIT)
    m = jnp.full((2 * tq, 1), -jnp.inf, F32)
    acc = jnp.zeros((2 * tq, DIFF_DV + LANES), F32)
    for start, size in key_chunks:
        k = k_ref[0, 0, start:start + size, :]
        v = jnp.concatenate([v_ref[0, 0, start:start + size, :], jnp.ones((size, LANES), BF16)],
                            axis=1)
        s = lax.dot_general(q2, k, _NT, preferred_element_type=F32)
        m_new = jnp.maximum(m, jnp.max(s, axis=-1, keepdims=True))
        p = jnp.exp2(s - m_new).astype(BF16)
        pv = jnp.concatenate([jnp.dot(p[:tq], v, preferred_element_type=F32),
                              jnp.dot(p[tq:], v, preferred_element_type=F32)], axis=0)
        acc = jnp.exp2(m - m_new) * acc + pv
        m = m_new
    pv = acc[:, :DIFF_DV] / acc[:, DIFF_DV:]
    o = pv[:tq] - lam * pv[tq:]
    ms = jnp.mean(o * o, axis=-1, keepdims=True)
    o_ref[0] = (o * lax.rsqrt(ms + 1e-5) * nw_ref[...] * (1.0 - LAM_INIT)).astype(BF16)


def _attention(qr, kr, vr, lam8, diff_norm_w, s):
    bsz, nh, t, _ = kr.shape
    tq = _tile(s, 512, ROW_BLOCK)
    kc = _tile(s, 2048, ROW_BLOCK)
    key_chunks = [(st, kc) for st in range(0, s, kc)] + [(s, t - s)]
    return pl.pallas_call(
        functools.partial(_attn_kernel, key_chunks=tuple(key_chunks)),
        grid=(bsz, nh, s // tq),
        in_specs=[pl.BlockSpec((1, 1, tq, LANES), lambda b, h, i: (b, h, i, 0)),
                  pl.BlockSpec((1, 1, t, LANES), lambda b, h, i: (b, h, 0, 0)),
                  pl.BlockSpec((1, 1, t, LANES), lambda b, h, i: (b, h, 0, 0)),
                  pl.BlockSpec((SUBLANES, LANES), lambda b, h, i: (0, 0)),
                  pl.BlockSpec((1, DIFF_DV), lambda b, h, i: (0, 0))],
        out_specs=pl.BlockSpec((1, tq, DIFF_DV), lambda b, h, i: (b, i, h)),
        out_shape=jax.ShapeDtypeStruct((bsz, s, nh * DIFF_DV), BF16),
        compiler_params=_params("parallel", "parallel", "arbitrary"),
        name="diff_attn",
    )(qr, kr, vr, lam8, diff_norm_w.reshape(1, DIFF_DV))


def _merge_kernel(ya_ref, yb_ref, wa_ref, wb_ref, ga_ref, gb_ref, o_ref):
    sa = jax.nn.sigmoid(ga_ref[0])
    sb = jax.nn.sigmoid(gb_ref[0])
    pa = jnp.dot(ya_ref[0], wa_ref[...], preferred_element_type=F32)
    pb = jnp.dot(yb_ref[0], wb_ref[...], preferred_element_type=F32)
    o_ref[0] = (sa * pa + sb * pb).astype(BF16)


def _merge(y_a, y_b, wa, wb, p3, ga_col, gb_col):
    bsz, s, ka = y_a.shape
    kb = y_b.shape[2]
    d = wa.shape[1]
    tm = _tile(s, 1024, ROW_BLOCK)
    tn = _tile(math.gcd(d, ga_col, gb_col), 512, LANES)
    return pl.pallas_call(
        _merge_kernel,
        grid=(bsz, s // tm, d // tn),
        in_specs=[pl.BlockSpec((1, tm, ka), lambda b, i, j: (b, i, 0)),
                  pl.BlockSpec((1, tm, kb), lambda b, i, j: (b, i, 0)),
                  pl.BlockSpec((ka, tn), lambda b, i, j: (0, j)),
                  pl.BlockSpec((kb, tn), lambda b, i, j: (0, j)),
                  pl.BlockSpec((1, tm, tn), lambda b, i, j: (b, i, ga_col // tn + j)),
                  pl.BlockSpec((1, tm, tn), lambda b, i, j: (b, i, gb_col // tn + j))],
        out_specs=pl.BlockSpec((1, tm, tn), lambda b, i, j: (b, i, j)),
        out_shape=jax.ShapeDtypeStruct((bsz, s, d), BF16),
        compiler_params=_params("parallel", "parallel", "arbitrary"),
        name="merge",
    )(y_a, y_b, wa, wb, p3, p3)


LN_ROWS = 32
LN_ACCS = 4


def _layernorm_rows(ref, g_ref, b_ref, part_ref, mean_ref, rstd_ref, emit):
    tm, d = ref.shape
    n_col = d // LANES
    n_grp = tm // LN_ROWS

    def row_slice(r):
        return pl.ds(pl.multiple_of(r * LN_ROWS, LN_ROWS), LN_ROWS)

    def tree_sum(terms):
        accs = list(terms[:LN_ACCS])
        for n, term in enumerate(terms[LN_ACCS:]):
            accs[n % LN_ACCS] = accs[n % LN_ACCS] + term
        while len(accs) > 1:
            accs = [accs[n] + accs[n + 1] for n in range(0, len(accs), 2)]
        return accs[0]

    def sum_body(r, carry):
        rows = row_slice(r)
        part_ref[rows, :] = tree_sum([ref[rows, c * LANES:(c + 1) * LANES] for c in range(n_col)])
        return carry

    lax.fori_loop(0, n_grp, sum_body, 0)
    mean_ref[...] = jnp.broadcast_to(
        jnp.sum(part_ref[...], axis=-1, keepdims=True) * (1.0 / d), mean_ref.shape)

    def var_body(r, carry):
        rows = row_slice(r)
        mu = mean_ref[rows, :]
        sq = []
        for c in range(n_col):
            xc = ref[rows, c * LANES:(c + 1) * LANES] - mu
            sq.append(xc * xc)
        part_ref[rows, :] = tree_sum(sq)
        return carry

    lax.fori_loop(0, n_grp, var_body, 0)
    var = jnp.sum(part_ref[...], axis=-1, keepdims=True) * (1.0 / d)
    rstd_ref[...] = jnp.broadcast_to(lax.rsqrt(var + LN_EPS), rstd_ref.shape)

    def norm_body(r, carry):
        rows = row_slice(r)
        mu = mean_ref[rows, :]
        rs = rstd_ref[rows, :]
        for c in range(n_col):
            cols = slice(c * LANES, (c + 1) * LANES)
            y = (ref[rows, cols] - mu) * rs * g_ref[:, cols] + b_ref[:, cols]
            ref[rows, cols] = y
            emit(rows, cols, y)
        return carry

    lax.fori_loop(0, n_grp, norm_body, 0)


def _proj_ln_kernel(a_ref, w_ref, x_ref, gt_ref, g_ref, b_ref, sc_ref, sh_ref,
                    o_ref, h_ref, part_ref, mean_ref, rstd_ref, mod_ref, *, alpha, n_j, tn,
                    rows_per_batch):
    i = pl.program_id(0)
    j = pl.program_id(1)
    tm = a_ref.shape[0]
    b = (i * tm) // rows_per_batch
    mix = jnp.dot(a_ref[...], w_ref[...], preferred_element_type=F32)
    r = alpha * x_ref[...] + gt_ref[pl.ds(b, 1), :] * mix
    for jj in range(n_j):
        @pl.when(j == jj)
        def _(jj=jj):
            o_ref[:, jj * tn:(jj + 1) * tn] = r

    @pl.when(j == n_j - 1)
    def _():
        if h_ref is None:
            emit = lambda rows, cols, y: None
        else:
            mod_ref[0:1, :] = 1.0 + sc_ref[pl.ds(b, 1), :]
            mod_ref[1:2, :] = sh_ref[pl.ds(b, 1), :]

            def emit(rows, cols, y):
                h_ref[rows, cols] = (y * mod_ref[0:1, cols] + mod_ref[1:2, cols]).astype(BF16)

        _layernorm_rows(o_ref, g_ref, b_ref, part_ref, mean_ref, rstd_ref, emit)


def _proj_ln_kernel_noh(a_ref, w_ref, x_ref, gt_ref, g_ref, b_ref, o_ref,
                        part_ref, mean_ref, rstd_ref, mod_ref, **kw):
    _proj_ln_kernel(a_ref, w_ref, x_ref, gt_ref, g_ref, b_ref, None, None, o_ref, None,
                    part_ref, mean_ref, rstd_ref, mod_ref, **kw)


def _proj_ln(a, w, x, mod, gt_chunk, ln_g, ln_b, rows_per_batch, *, alpha, tm, tn, name,
             next_mod_chunks=None):
    m, k = a.shape
    d = w.shape[1]
    n_j = d // tn
    kw = dict(alpha=alpha, n_j=n_j, tn=tn, rows_per_batch=rows_per_batch)
    in_specs = [pl.BlockSpec((tm, k), lambda i, j: (i, 0)),
                pl.BlockSpec((k, tn), lambda i, j: (0, j)),
                pl.BlockSpec((tm, tn), lambda i, j: (i, j)),
                pl.BlockSpec((SUBLANES, tn), lambda i, j: (0, gt_chunk * n_j + j)),
                pl.BlockSpec((1, d), lambda i, j: (0, 0)),
                pl.BlockSpec((1, d), lambda i, j: (0, 0))]
    args = [a, w, x, mod, ln_g.reshape(1, d), ln_b.reshape(1, d)]
    row_spec = pl.BlockSpec((tm, d), lambda i, j: (i, 0))
    if next_mod_chunks is None:
        kernel = functools.partial(_proj_ln_kernel_noh, **kw)
        out_specs = row_spec
        out_shape = jax.ShapeDtypeStruct((m, d), F32)
    else:
        sh_chunk, sc_chunk = next_mod_chunks
        kernel = functools.partial(_proj_ln_kernel, **kw)
        in_specs += [pl.BlockSpec((SUBLANES, d), lambda i, j: (0, sc_chunk)),
                     pl.BlockSpec((SUBLANES, d), lambda i, j: (0, sh_chunk))]
        args += [mod, mod]
        out_specs = [row_spec, row_spec]
        out_shape = [jax.ShapeDtypeStruct((m, d), F32), jax.ShapeDtypeStruct((m, d), BF16)]
    return pl.pallas_call(
        kernel,
        grid=(m // tm, n_j),
        in_specs=in_specs,
        out_specs=out_specs,
        out_shape=out_shape,
        scratch_shapes=[pltpu.VMEM((tm, LANES), F32), pltpu.VMEM((tm, LANES), F32),
                        pltpu.VMEM((tm, LANES), F32), pltpu.VMEM((SUBLANES, d), F32)],
        compiler_params=_params("parallel", "arbitrary"),
        name=name,
    )(*args)


def _ffn_up_kernel(h_ref, wg_ref, wu_ref, o_ref):
    h = h_ref[...]
    g = jnp.dot(h, wg_ref[...].astype(BF16), preferred_element_type=F32)
    u = jnp.dot(h, wu_ref[...].astype(BF16), preferred_element_type=F32)
    o_ref[...] = (_silu(g) * u).astype(BF16)


def _ffn_up(h, wg, wu):
    m, d = h.shape
    f = wg.shape[1]
    tm = _tile(m, 2048, ROW_BLOCK)
    tn = _tile(f, 512, LANES)
    return pl.pallas_call(
        _ffn_up_kernel,
        grid=(m // tm, f // tn),
        in_specs=[pl.BlockSpec((tm, d), lambda i, j: (i, 0)),
                  pl.BlockSpec((d, tn), lambda i, j: (0, j)),
                  pl.BlockSpec((d, tn), lambda i, j: (0, j))],
        out_specs=pl.BlockSpec((tm, tn), lambda i, j: (i, j)),
        out_shape=jax.ShapeDtypeStruct((m, f), BF16),
        compiler_params=_params("parallel", "arbitrary"),
        name="ffn_up",
    )(h, wg, wu)


def _rope_tables(s, c):
    quarter = DIFF_DQK // 4
    inv_freq = ROPE_BASE ** (-jnp.arange(quarter, dtype=F32) / quarter)
    pos = jnp.arange(s, dtype=jnp.int32)
    lane = jnp.arange(LANES)
    use_col = (lane % DIFF_DQK) >= (DIFF_DQK // 2)
    p = jnp.where(use_col[None, :], (pos % GRID_W)[:, None], (pos // GRID_W)[:, None]).astype(F32)
    ang = p * inv_freq[lane % quarter][None, :]
    sign = jnp.where((lane % (2 * quarter)) < quarter, -1.0, 1.0)[None, :]
    cos = jnp.concatenate([jnp.cos(ang), jnp.ones((c, LANES), F32)], axis=0)
    sin = jnp.concatenate([jnp.sin(ang) * sign, jnp.zeros((c, LANES), F32)], axis=0)
    return cos, sin


def kernel(x, c, ctx, c_ctx, w_ada, b_ada, w_in, conv_w, a_log, dt_bias, gdn_norm_w, lam_q1, lam_k1,
           lam_q2, lam_k2, diff_norm_w, w_proj_a, w_proj_b, w_out, ln1_g, ln1_b, w_gate, w_up, w_down,
           ln2_g, ln2_b):
    bsz, s, d = x.shape
    cl = ctx.shape[1]
    t = s + cl
    nh = d // HEAD_W
    qk_w = nh * GDN_DK
    v_w = nh * GDN_DV
    dqk_w = nh * 2 * DIFF_DQK
    dv_w = nh * DIFF_DV
    alpha = (2.0 * w_ada.shape[0]) ** 0.25
    assert w_ada.shape[0] == 1 and s % ROW_BLOCK == 0 and cl % ROW_BLOCK == 0 and 8 * nh <= LANES

    c8 = jnp.zeros((SUBLANES, d), F32).at[:bsz].set(c).at[bsz].set(c_ctx)
    mod = _ada(c8, w_ada[0], b_ada[0])

    h_all = _modulate(x, ctx, mod, bsz)

    w_l = jnp.swapaxes(w_in[0], 0, 1)
    z_col = 2 * qk_w + v_w
    ab_start = z_col + v_w
    d_start = ab_start + 4 * nh
    n_d = 2 * dqk_w + dv_w + 2 * d
    h2d = h_all.reshape(bsz * t, d)
    tm_in = _tile(bsz * t, 512, ROW_BLOCK)
    p_g = _in_proj(h2d, w_l, 0, ab_start, tm=tm_in, tn=_tile(ab_start, 1024, LANES),
                   name="in_proj_gdn").reshape(bsz, t, ab_start)
    p_ab = _in_proj(h2d, w_l, ab_start, LANES, tm=tm_in, tn=LANES,
                    name="in_proj_ab").reshape(bsz, t, LANES)
    p_d = _in_proj(h2d, w_l, d_start, n_d, tm=tm_in, tn=_tile(n_d, 1024, LANES),
                   name="in_proj_diff").reshape(bsz, t, n_d)
    ga_col = 2 * dqk_w + dv_w
    gb_col = ga_col + d

    w2 = 2 * nh
    alog128 = jnp.pad(a_log[0].reshape(1, w2), ((0, 0), (0, LANES - w2)))
    dtb128 = jnp.pad(dt_bias[0].reshape(1, w2), ((0, 0), (0, LANES - w2)))
    gb = _gates(p_ab, alog128, dtb128, w2)
    gr = jnp.swapaxes(gb[:, :, :w2], 1, 2)
    g_tot = gb[:, ::GDN_CHUNK, 3 * w2:4 * w2].reshape(-1)
    qkv = _gdn_conv(p_g, conv_w[0], nh, s)
    u, w_, qd, qk, kdt = _gdn_intra(qkv, gb, gr, nh)
    o_f, o_b = _gdn_scan(g_tot, u, w_, qd, qk, kdt, nh, s)
    y_a = _gdn_out(o_f, o_b, p_g, z_col // v_w, gdn_norm_w[0], nh, s)

    cos_t, sin_t = _rope_tables(s, cl)
    qr, kr, vr = _rope(p_d, cos_t, sin_t, nh)
    lam8 = jnp.zeros((SUBLANES, LANES), F32)
    for r, vec in enumerate((lam_q1, lam_k1, lam_q2, lam_k2)):
        lam8 = lam8.at[r, :DIFF_DQK].set(vec[0])
    y_b = _attention(qr, kr, vr, lam8, diff_norm_w[0], s)

    ymg = _merge(y_a, y_b, w_proj_a[0].astype(BF16), w_proj_b[0].astype(BF16), p_d, ga_col, gb_col)
    m = bsz * s
    x1, h2 = _proj_ln(ymg.reshape(m, d), w_out[0].astype(BF16), x.reshape(m, d), mod, 2,
                      ln1_g[0], ln1_b[0], s, alpha=alpha, tm=_tile(s, 512, ROW_BLOCK),
                      tn=_tile(d, 512, LANES), name="out_proj_ln", next_mod_chunks=(3, 4))

    hff = _ffn_up(h2, w_gate[0], w_up[0])
    out = _proj_ln(hff, w_down[0].astype(BF16), x1, mod, 5, ln2_g[0], ln2_b[0], s, alpha=alpha,
                   tm=_tile(s, 512, ROW_BLOCK), tn=_tile(d, 256, LANES), name="ffn_down_ln")
    return out.reshape(bsz, s, d)
```
